```python
import math
import jax, jax.numpy as jnp
from jax import lax
import numpy as np

D_MODEL = 1024
BATCH = 4
SEQ = 8192
DEPTH = 2

ROPE_THETA = 10000.0
NORM_EPS = 1e-6
Q_BLOCK = 128

MOBA_HEADS = 8
MOBA_HEAD_DIM = 64
MOBA_BLOCK = 256
MOBA_TOPK = 3
MOBA_Q_CHUNK = 32

DIFF_HEADS = 4
DIFF_HEAD_DIM = 64
DIFF_V_DIM = 2 * DIFF_HEAD_DIM

MOBA_W = MOBA_HEADS * MOBA_HEAD_DIM
DIFF_QK_W = DIFF_HEADS * 2 * DIFF_HEAD_DIM
DIFF_V_W = DIFF_HEADS * DIFF_V_DIM
EVEN_IN = 3 * MOBA_W + 2 * DIFF_QK_W + DIFF_V_W
EVEN_MIX = MOBA_W + DIFF_V_W
EVEN_SPLITS = (MOBA_W, 2 * MOBA_W, 3 * MOBA_W, 3 * MOBA_W + DIFF_QK_W, 3 * MOBA_W + 2 * DIFF_QK_W)

MLA_HEADS = 16
MLA_Q_RANK = 256
MLA_KV_RANK = 128
MLA_NOPE = 64
MLA_ROPE = 32
MLA_V = 64
ODD_IN = MLA_Q_RANK + MLA_KV_RANK + MLA_ROPE
ODD_MIX = MLA_HEADS * MLA_V

N_GROUPS = 4
EXPERTS_PER_GROUP = 8
N_EXPERTS = N_GROUPS * EXPERTS_PER_GROUP
EXPERT_TOPK = 2
EXPERT_FF = 256

N_EVEN = (DEPTH + 1) // 2
N_ODD = DEPTH // 2

kernel_name = "hybrid_moba_diffattn_mla_hiermoe"


def rms_norm(x, g):
    xf = x.astype(jnp.float32)
    y = xf * lax.rsqrt(jnp.mean(xf * xf, axis=-1, keepdims=True) + NORM_EPS)
    return (y * g.astype(jnp.float32)).astype(x.dtype)


def rope_tables(seq, dim):
    inv = 1.0 / (ROPE_THETA ** (jnp.arange(0, dim, 2, dtype=jnp.float32) / dim))
    ang = jnp.arange(seq, dtype=jnp.float32)[:, None] * inv[None, :]
    return jnp.cos(ang), jnp.sin(ang)


def apply_rope(x, cos, sin):
    shp = (1, cos.shape[0]) + (1,) * (x.ndim - 3) + (cos.shape[1],)
    c = cos.reshape(shp).astype(x.dtype)
    s = sin.reshape(shp).astype(x.dtype)
    h = x.shape[-1] // 2
    x1, x2 = x[..., :h], x[..., h:]
    return jnp.concatenate([x1 * c - x2 * s, x1 * s + x2 * c], axis=-1)


def causal_attention(q, k, v, scale):
    B, S, H, dk = q.shape
    nq = S // Q_BLOCK
    qb = q.reshape(B, nq, Q_BLOCK, H, dk).transpose(1, 0, 3, 2, 4)
    kt = k.transpose(0, 2, 1, 3)
    vt = v.transpose(0, 2, 1, 3)
    kpos = jnp.arange(S)

    def step(args):
        q_blk, i = args
        qpos = i * Q_BLOCK + jnp.arange(Q_BLOCK)
        s = jnp.einsum('bhqd,bhkd->bhqk', q_blk, kt).astype(jnp.float32) * scale
        s = jnp.where(kpos[None, :] <= qpos[:, None], s, -jnp.inf)
        p = jax.nn.softmax(s, axis=-1).astype(v.dtype)
        return jnp.einsum('bhqk,bhkd->bqhd', p, vt)

    o = lax.map(step, (qb, jnp.arange(nq)))
    return o.transpose(1, 0, 2, 3, 4).reshape(B, S, H, v.shape[-1])


def diff_attention(q1, q2, k1, k2, v, lam):
    B, S, H, d = q1.shape
    scale = d ** -0.5
    nq = S // Q_BLOCK

    def to_blocks(q):
        return q.reshape(B, nq, Q_BLOCK, H, d).transpose(1, 0, 3, 2, 4)

    k1t = k1.transpose(0, 2, 1, 3)
    k2t = k2.transpose(0, 2, 1, 3)
    vt = v.transpose(0, 2, 1, 3)
    kpos = jnp.arange(S)

    def step(args):
        q1b, q2b, i = args
        qpos = i * Q_BLOCK + jnp.arange(Q_BLOCK)
        causal = kpos[None, :] <= qpos[:, None]
        s1 = jnp.where(causal, jnp.einsum('bhqd,bhkd->bhqk', q1b, k1t).astype(jnp.float32) * scale, -jnp.inf)
        s2 = jnp.where(causal, jnp.einsum('bhqd,bhkd->bhqk', q2b, k2t).astype(jnp.float32) * scale, -jnp.inf)
        a = jax.nn.softmax(s1, axis=-1) - lam * jax.nn.softmax(s2, axis=-1)
        return jnp.einsum('bhqk,bhkd->bqhd', a.astype(v.dtype), vt)

    o = lax.map(step, (to_blocks(q1), to_blocks(q2), jnp.arange(nq)))
    return o.transpose(1, 0, 2, 3, 4).reshape(B, S, H, v.shape[-1])


def moba_attention(q, k, v):
    B, S, H, d = q.shape
    L = MOBA_BLOCK
    nb = -(-S // L)
    pad = nb * L - S
    kp = jnp.pad(k, ((0, 0), (0, pad), (0, 0), (0, 0)))
    vp = jnp.pad(v, ((0, 0), (0, pad), (0, 0), (0, 0)))
    kb = kp.reshape(B, nb, L, H, d).transpose(0, 3, 1, 2, 4)
    vb = vp.reshape(B, nb, L, H, d).transpose(0, 3, 1, 2, 4)
    kmean = jnp.mean(kb.astype(jnp.float32), axis=3).astype(k.dtype)
    topk_n = max(1, min(MOBA_TOPK, nb - 1))
    nc = S // MOBA_Q_CHUNK
    qc = q.reshape(B, nc, MOBA_Q_CHUNK, H, d).transpose(1, 0, 3, 2, 4)
    scale = d ** -0.5
    bidx = jnp.arange(nb)
    b_ar = jnp.arange(B)[:, None, None, None]
    h_ar = jnp.arange(H)[None, :, None, None]

    def step(args):
        q_blk, c = args
        qpos = c * MOBA_Q_CHUNK + jnp.arange(MOBA_Q_CHUNK)
        own = qpos[0] // L
        g = jnp.einsum('bhqd,bhnd->bhqn', q_blk, kmean).astype(jnp.float32)
        past = bidx[None, :] < (qpos // L)[:, None]
        g = jnp.where(past, g, -jnp.inf)
        gv, sel = lax.top_k(g, topk_n)
        valid = jnp.isfinite(gv)
        k_sel = kb[b_ar, h_ar, sel]
        v_sel = vb[b_ar, h_ar, sel]
        s_sel = jnp.einsum('bhqd,bhqnld->bhqnl', q_blk, k_sel).astype(jnp.float32) * scale
        s_sel = jnp.where(valid[..., None], s_sel, -jnp.inf).reshape(B, H, MOBA_Q_CHUNK, topk_n * L)
        k_own = lax.dynamic_index_in_dim(kb, own, axis=2, keepdims=False)
        v_own = lax.dynamic_index_in_dim(vb, own, axis=2, keepdims=False)
        s_own = jnp.einsum('bhqd,bhld->bhql', q_blk, k_own).astype(jnp.float32) * scale
        own_pos = own * L + jnp.arange(L)
        s_own = jnp.where(own_pos[None, :] <= qpos[:, None], s_own, -jnp.inf)
        p = jax.nn.softmax(jnp.concatenate([s_sel, s_own], axis=-1), axis=-1).astype(v.dtype)
        p_sel = p[..., :topk_n * L].reshape(B, H, MOBA_Q_CHUNK, topk_n, L)
        p_own = p[..., topk_n * L:]
        return (jnp.einsum('bhqnl,bhqnld->bqhd', p_sel, v_sel)
                + jnp.einsum('bhql,bhld->bqhd', p_own, v_own))

    o = lax.map(step, (qc, jnp.arange(nc)))
    return o.transpose(1, 0, 2, 3, 4).reshape(B, S, H, d)


def even_mixer(h, w_in, lq1, lk1, lq2, lk2, subln_g, w_out, lam_init, cos, sin):
    B, S, _ = h.shape
    z = h @ w_in
    mq, mk, mv, dq, dk, dv = jnp.split(z, EVEN_SPLITS, axis=-1)
    mq = apply_rope(mq.reshape(B, S, MOBA_HEADS, MOBA_HEAD_DIM), cos, sin)
    mk = apply_rope(mk.reshape(B, S, MOBA_HEADS, MOBA_HEAD_DIM), cos, sin)
    mv = mv.reshape(B, S, MOBA_HEADS, MOBA_HEAD_DIM)
    o_a = moba_attention(mq, mk, mv).reshape(B, S, MOBA_W)

    dq = apply_rope(dq.reshape(B, S, DIFF_HEADS, 2, DIFF_HEAD_DIM), cos, sin)
    dk = apply_rope(dk.reshape(B, S, DIFF_HEADS, 2, DIFF_HEAD_DIM), cos, sin)
    dv = dv.reshape(B, S, DIFF_HEADS, DIFF_V_DIM)
    f32 = jnp.float32
    lam = (jnp.exp(jnp.sum(lq1.astype(f32) * lk1.astype(f32)))
           - jnp.exp(jnp.sum(lq2.astype(f32) * lk2.astype(f32))) + lam_init)
    o_b = diff_attention(dq[..., 0, :], dq[..., 1, :], dk[..., 0, :], dk[..., 1, :], dv, lam)
    o_b = (rms_norm(o_b, subln_g) * (1.0 - lam_init)).reshape(B, S, DIFF_V_W)
    return jnp.concatenate([o_a, o_b], axis=-1) @ w_out


def mla_mixer(h, w_in, q_norm_g, kv_norm_g, w_q_up, w_kv_up, w_out, cos, sin):
    B, S, _ = h.shape
    z = h @ w_in
    cq, ckv, kr = jnp.split(z, (MLA_Q_RANK, MLA_Q_RANK + MLA_KV_RANK), axis=-1)
    q = (rms_norm(cq, q_norm_g) @ w_q_up).reshape(B, S, MLA_HEADS, MLA_NOPE + MLA_ROPE)
    q = jnp.concatenate([q[..., :MLA_NOPE], apply_rope(q[..., MLA_NOPE:], cos, sin)], axis=-1)
    kv = (rms_norm(ckv, kv_norm_g) @ w_kv_up).reshape(B, S, MLA_HEADS, MLA_NOPE + MLA_V)
    k_rope = apply_rope(kr[:, :, None, :], cos, sin)
    k = jnp.concatenate([kv[..., :MLA_NOPE],
                         jnp.broadcast_to(k_rope, (B, S, MLA_HEADS, MLA_ROPE))], axis=-1)
    v = kv[..., MLA_NOPE:]
    o = causal_attention(q, k, v, (MLA_NOPE + MLA_ROPE) ** -0.5)
    return o.reshape(B, S, ODD_MIX) @ w_out


def hier_moe(h, w_group, b_group, w_expert, b_expert, w_gate, w_up, w_down):
    B, S, D = h.shape
    t = h.reshape(-1, D)
    N = t.shape[0]
    gp = jax.nn.softmax((t @ w_group).astype(jnp.float32) + b_group.astype(jnp.float32), axis=-1)
    grp = jnp.argmax(gp, axis=-1)
    pg = jnp.take_along_axis(gp, grp[:, None], axis=1)[:, 0]
    el = ((t @ w_expert).astype(jnp.float32) + b_expert.astype(jnp.float32)).reshape(N, N_GROUPS, EXPERTS_PER_GROUP)
    el_g = jnp.take_along_axis(el, grp[:, None, None], axis=1)[:, 0]
    pv, pi = lax.top_k(jax.nn.softmax(el_g, axis=-1), EXPERT_TOPK)
    w = pg[:, None] * pv / jnp.sum(pv, axis=-1, keepdims=True)
    eid = grp[:, None] * EXPERTS_PER_GROUP + pi
    gates = jnp.zeros((N, N_EXPERTS), jnp.float32).at[jnp.arange(N)[:, None], eid].add(w)

    def expert_step(acc, params):
        wg, wu, wd, ge = params
        hh = jax.nn.silu(t @ wg) * (t @ wu)
        return acc + ge[:, None].astype(t.dtype) * (hh @ wd), None

    out, _ = lax.scan(expert_step, jnp.zeros_like(t), (w_gate, w_up, w_down, gates.T))
    return out.reshape(B, S, D)


def setup_inputs(seed: int = 0) -> dict:
    key = jax.random.key(seed)
    ks = jax.random.split(key, 32)
    D = D_MODEL

    def nrm(k, shape, scale):
        return jax.random.normal(k, shape, jnp.float32) * scale

    return {
        "x": nrm(ks[0], (BATCH, SEQ, D), 1.0),
        "attn_norm": 1.0 + nrm(ks[1], (DEPTH, D), 0.02),
        "ev_w_in": nrm(ks[2], (N_EVEN, D, EVEN_IN), D ** -0.5),
        "ev_lambda_q1": nrm(ks[3], (N_EVEN, DIFF_HEAD_DIM), 0.1),
        "ev_lambda_k1": nrm(ks[4], (N_EVEN, DIFF_HEAD_DIM), 0.1),
        "ev_lambda_q2": nrm(ks[5], (N_EVEN, DIFF_HEAD_DIM), 0.1),
        "ev_lambda_k2": nrm(ks[6], (N_EVEN, DIFF_HEAD_DIM), 0.1),
        "ev_subln": 1.0 + nrm(ks[7], (N_EVEN, DIFF_V_DIM), 0.02),
        "ev_w_out": nrm(ks[8], (N_EVEN, EVEN_MIX, D), EVEN_MIX ** -0.5),
        "od_w_in": nrm(ks[9], (N_ODD, D, ODD_IN), D ** -0.5),
        "od_q_norm": 1.0 + nrm(ks[10], (N_ODD, MLA_Q_RANK), 0.02),
        "od_kv_norm": 1.0 + nrm(ks[11], (N_ODD, MLA_KV_RANK), 0.02),
        "od_w_q_up": nrm(ks[12], (N_ODD, MLA_Q_RANK, MLA_HEADS * (MLA_NOPE + MLA_ROPE)), MLA_Q_RANK ** -0.5),
        "od_w_kv_up": nrm(ks[13], (N_ODD, MLA_KV_RANK, MLA_HEADS * (MLA_NOPE + MLA_V)), MLA_KV_RANK ** -0.5),
        "od_w_out": nrm(ks[14], (N_ODD, ODD_MIX, D), ODD_MIX ** -0.5),
        "ffn_norm": 1.0 + nrm(ks[15], (DEPTH, D), 0.02),
        "moe_w_group": nrm(ks[16], (DEPTH, D, N_GROUPS), D ** -0.5),
        "moe_b_group": nrm(ks[17], (DEPTH, N_GROUPS), 0.01),
        "moe_w_expert": nrm(ks[18], (DEPTH, D, N_EXPERTS), D ** -0.5),
        "moe_b_expert": nrm(ks[19], (DEPTH, N_EXPERTS), 0.01),
        "moe_w_gate": nrm(ks[20], (DEPTH, N_EXPERTS, D, EXPERT_FF), D ** -0.5),
        "moe_w_up": nrm(ks[21], (DEPTH, N_EXPERTS, D, EXPERT_FF), D ** -0.5),
        "moe_w_down": nrm(ks[22], (DEPTH, N_EXPERTS, EXPERT_FF, D), EXPERT_FF ** -0.5),
        "final_norm": 1.0 + nrm(ks[23], (D,), 0.02),
    }


def reference(x, attn_norm, ev_w_in, ev_lambda_q1, ev_lambda_k1, ev_lambda_q2, ev_lambda_k2,
              ev_subln, ev_w_out, od_w_in, od_q_norm, od_kv_norm, od_w_q_up, od_w_kv_up, od_w_out,
              ffn_norm, moe_w_group, moe_b_group, moe_w_expert, moe_b_expert, moe_w_gate,
              moe_w_up, moe_w_down, final_norm):
    S = x.shape[1]
    cos_h, sin_h = rope_tables(S, MOBA_HEAD_DIM)
    cos_r, sin_r = rope_tables(S, MLA_ROPE)
    for layer in range(DEPTH):
        j = layer // 2
        h = rms_norm(x, attn_norm[layer])
        if layer % 2 == 0:
            lam_init = 0.8 - 0.6 * math.exp(-0.3 * layer)
            x = x + even_mixer(h, ev_w_in[j], ev_lambda_q1[j], ev_lambda_k1[j], ev_lambda_q2[j],
                               ev_lambda_k2[j], ev_subln[j], ev_w_out[j], lam_init, cos_h, sin_h)
        else:
            x = x + mla_mixer(h, od_w_in[j], od_q_norm[j], od_kv_norm[j], od_w_q_up[j],
                              od_w_kv_up[j], od_w_out[j], cos_r, sin_r)
        x = x + hier_moe(rms_norm(x, ffn_norm[layer]), moe_w_group[layer], moe_b_group[layer],
                         moe_w_expert[layer], moe_b_expert[layer], moe_w_gate[layer],
                         moe_w_up[layer], moe_w_down[layer])
    return rms_norm(x, final_norm)
```

```python
import functools
import math

import jax
import jax.numpy as jnp
from jax import lax
from jax.experimental import pallas as pl
from jax.experimental.pallas import tpu as pltpu

D_MODEL = 1024
ROPE_THETA = 10000.0
NORM_EPS = 1e-6

HEAD_DIM = 64
MOBA_HEADS = 8
MOBA_BLOCK = 256
MOBA_TOPK = 3
DIFF_HEADS = 4
EVEN_IN = 3072

MLA_HEADS = 16
MLA_Q_RANK = 256
MLA_KV_RANK = 128
MLA_NOPE = 64
MLA_ROPE = 32
MLA_V = 64

N_GROUPS = 4
EXPERTS_PER_GROUP = 8
N_EXPERTS = 32
EXPERT_FF = 256

LANES = 128
MASK_BIAS = -1e30
VMEM_LIMIT = 48 * 1024 * 1024

TOKEN_TILE = 512
MOE_TILE = 1024
ATTN_TILE = 256

_F32 = jnp.float32
_BF16 = jnp.bfloat16


def _dot(a, b):
    return jnp.dot(a, b, preferred_element_type=_F32)


def _dot_nt(a, b):
    return lax.dot_general(a, b, (((1,), (1,)), ((), ())), preferred_element_type=_F32)


def _rms(x, g):
    return x * lax.rsqrt(jnp.mean(x * x, axis=-1, keepdims=True) + NORM_EPS) * g


def _params(*sem):
    return pltpu.CompilerParams(dimension_semantics=sem, vmem_limit_bytes=VMEM_LIMIT)


def _rope_tables(seq, dim):
    inv = 1.0 / (ROPE_THETA ** (jnp.arange(0, dim, 2, dtype=_F32) / dim))
    ang = jnp.arange(seq, dtype=_F32)[:, None] * inv[None, :]
    return jnp.cos(ang), jnp.sin(ang)


def _even_rope_tables(seq):
    cos, sin = _rope_tables(seq, HEAD_DIM)
    half = HEAD_DIM // 2
    lane = jnp.arange(LANES)
    first = (lane % HEAD_DIM) < half
    c = cos[:, lane % half]
    s = sin[:, lane % half]
    return c, jnp.where(first, -s, 0.0), jnp.where(first, 0.0, s)


def _mla_rope_tables(seq, scale):
    cos, sin = _rope_tables(seq, MLA_ROPE)
    half = MLA_ROPE // 2
    lane = jnp.arange(LANES)
    r = (lane - MLA_NOPE) % half
    in_rope = (lane >= MLA_NOPE) & (lane < MLA_NOPE + MLA_ROPE)
    first = (lane >= MLA_NOPE) & (lane < MLA_NOPE + half)
    second = (lane >= MLA_NOPE + half) & (lane < MLA_NOPE + MLA_ROPE)
    c, s = cos[:, r], sin[:, r]
    cq = jnp.where(lane < MLA_NOPE, 1.0, jnp.where(in_rope, c, 0.0)) * scale
    ck = jnp.where(in_rope, c, 0.0)
    sa = jnp.where(first, -s, 0.0)
    sb = jnp.where(second, s, 0.0)
    return cq, sa * scale, sb * scale, ck, sa, sb


def _rotate(z, c, sa, sb, half):
    return z * c + pltpu.roll(z, LANES - half, 1) * sa + pltpu.roll(z, half, 1) * sb


def _even_proj_kernel(x_ref, g_ref, w_ref, c_ref, sa_ref, sb_ref, o_ref):
    h = _rms(x_ref[...], g_ref[...]).astype(_BF16)
    c, sa, sb = c_ref[...], sa_ref[...], sb_ref[...]
    chunk = 512
    for ci in range(EVEN_IN // chunk):
        z = _dot(h, w_ref[:, ci * chunk:(ci + 1) * chunk])
        roped = ci in (0, 1, 3, 4)
        is_q = ci in (0, 3)
        for b in range(chunk // LANES):
            zz = z[:, b * LANES:(b + 1) * LANES]
            if roped:
                zz = _rotate(zz, c, sa, sb, HEAD_DIM // 2)
            if is_q:
                zz = zz * (HEAD_DIM ** -0.5)
            o_ref[:, ci * chunk + b * LANES:ci * chunk + (b + 1) * LANES] = zz.astype(_BF16)


def _even_proj(x2, g, w, tabs, seq):
    n = x2.shape[0]
    tm = TOKEN_TILE
    spb = seq // tm
    tab_spec = pl.BlockSpec((tm, LANES), lambda i: (i % spb, 0))
    return pl.pallas_call(
        _even_proj_kernel,
        grid=(n // tm,),
        in_specs=[
            pl.BlockSpec((tm, D_MODEL), lambda i: (i, 0)),
            pl.BlockSpec((1, D_MODEL), lambda i: (0, 0)),
            pl.BlockSpec((D_MODEL, EVEN_IN), lambda i: (0, 0)),
            tab_spec, tab_spec, tab_spec,
        ],
        out_specs=pl.BlockSpec((tm, EVEN_IN), lambda i: (i, 0)),
        out_shape=jax.ShapeDtypeStruct((n, EVEN_IN), _BF16),
        compiler_params=_params("parallel"),
        name="even_proj",
    )(x2, g, w, *tabs)


def _softmax_step(t, s, v, m_ref, l_ref, acc_ref, first):
    row_max = jnp.max(s, axis=1, keepdims=True)
    if first:
        m_new = row_max
        p = jnp.exp(s - m_new)
        l_new = jnp.sum(p, axis=1, keepdims=True)
        acc = _dot(p.astype(_BF16), v)
    else:
        m_prev = m_ref[t][:, :1]
        l_prev = l_ref[t][:, :1]
        m_new = jnp.maximum(m_prev, row_max)
        alpha = jnp.exp(m_prev - m_new)
        p = jnp.exp(s - m_new)
        l_new = alpha * l_prev + jnp.sum(p, axis=1, keepdims=True)
        acc = alpha * acc_ref[t] + _dot(p.astype(_BF16), v)
    m_ref[t] = jnp.broadcast_to(m_new, m_ref.shape[1:])
    l_ref[t] = jnp.broadcast_to(l_new, l_ref.shape[1:])
    acc_ref[t] = acc


def _causal(s):
    row = lax.broadcasted_iota(jnp.int32, s.shape, 0)
    col = lax.broadcasted_iota(jnp.int32, s.shape, 1)
    return jnp.where(col <= row, s, -jnp.inf)


def _kv_tile(ref, j):
    return ref[pl.ds(pl.multiple_of(j * ATTN_TILE, ATTN_TILE), ATTN_TILE), :]


def _normalised(acc_ref, l_ref, t):
    return acc_ref[t] / l_ref[t][:, :1]


def _moba_kernel(q_ref, k_ref, v_ref, o_ref, km_ref, m_ref, l_ref, acc_ref, *, nb, topk):
    qi = pl.program_id(2)
    tq = ATTN_TILE
    lane = lax.broadcasted_iota(jnp.int32, (tq, LANES), 1)
    lane_f = lane.astype(_F32)
    lo = lane < HEAD_DIM

    @pl.when(qi == 0)
    def _():
        kf = k_ref[...].astype(_F32)
        means = jnp.sum(kf.reshape(nb, MOBA_BLOCK, LANES), axis=1) * (1.0 / MOBA_BLOCK)
        lane_b = lax.broadcasted_iota(jnp.int32, (nb, LANES), 1)
        km_ref[...] = jnp.zeros(km_ref.shape, _F32)
        km_ref[0:nb, :] = jnp.where(lane_b < HEAD_DIM, 0.0, means)
        km_ref[HEAD_DIM:HEAD_DIM + nb, :] = jnp.where(lane_b < HEAD_DIM, means, 0.0)

    q = q_ref[...]
    gate = _dot_nt(q, km_ref[...].astype(_BF16))

    def block_bias(base):
        in_blocks = (lane >= base) & (lane < base + HEAD_DIM // 2)
        g = jnp.where((lane >= base) & (lane < base + qi), gate, -jnp.inf)
        bias = jnp.where(in_blocks, MASK_BIAS, 0.0)
        for _ in range(topk):
            mx = jnp.max(g, axis=1, keepdims=True)
            hit = (g == mx) & (mx > -jnp.inf)
            idx = jnp.min(jnp.where(hit, lane_f, 4.0 * LANES), axis=1, keepdims=True)
            pick = lane_f == idx
            bias = jnp.where(pick, 0.0, bias)
            g = jnp.where(pick, -jnp.inf, g)
        return bias.astype(_BF16)

    zero = jnp.zeros_like(q)
    lo_b = jnp.where(lo, 1.0, 0.0).astype(_BF16)
    hi_b = jnp.where(lo, 0.0, 1.0).astype(_BF16)
    qs = (jnp.where(lo, q, block_bias(HEAD_DIM)), jnp.where(lo, block_bias(0), q))
    qd = (jnp.where(lo, q, zero), jnp.where(lo, zero, q))

    k_own = _kv_tile(k_ref, qi)
    v_own = _kv_tile(v_ref, qi)
    for t in range(2):
        _softmax_step(t, _causal(_dot_nt(qd[t], k_own)), v_own, m_ref, l_ref, acc_ref, True)

    def body(j, carry):
        k = _kv_tile(k_ref, j)
        v = _kv_tile(v_ref, j)
        jf = j.astype(_F32)
        e0 = jnp.where(lane_f == jf + HEAD_DIM, 1.0, 0.0).astype(_BF16)
        e1 = jnp.where(lane_f == jf, 1.0, 0.0).astype(_BF16)
        ks = (k * lo_b + e0, k * hi_b + e1)
        for t in range(2):
            _softmax_step(t, _dot_nt(qs[t], ks[t]), v, m_ref, l_ref, acc_ref, False)
        return carry

    lax.fori_loop(0, qi, body, 0)
    o_ref[...] = jnp.where(lo, _normalised(acc_ref, l_ref, 0),
                           _normalised(acc_ref, l_ref, 1)).astype(o_ref.dtype)


def _moba(z3):
    bsz, seq, _ = z3.shape
    tq = ATTN_TILE
    assert MOBA_BLOCK == tq and seq % MOBA_BLOCK == 0
    nb = seq // MOBA_BLOCK
    assert nb <= HEAD_DIM // 2
    topk = max(1, min(MOBA_TOPK, nb - 1))
    pairs = MOBA_HEADS // 2
    return pl.pallas_call(
        functools.partial(_moba_kernel, nb=nb, topk=topk),
        grid=(bsz, pairs, seq // tq),
        in_specs=[
            pl.BlockSpec((None, tq, LANES), lambda b, p, i: (b, i, p)),
            pl.BlockSpec((None, seq, LANES), lambda b, p, i: (b, 0, pairs + p)),
            pl.BlockSpec((None, seq, LANES), lambda b, p, i: (b, 0, 2 * pairs + p)),
        ],
        out_specs=pl.BlockSpec((None, tq, LANES), lambda b, p, i: (b, i, p)),
        out_shape=jax.ShapeDtypeStruct((bsz, seq, pairs * LANES), _BF16),
        scratch_shapes=[
            pltpu.VMEM((LANES, LANES), _F32),
            pltpu.VMEM((2, tq, LANES), _F32),
            pltpu.VMEM((2, tq, LANES), _F32),
            pltpu.VMEM((2, tq, LANES), _F32),
        ],
        compiler_params=_params("parallel", "parallel", "arbitrary"),
        name="moba_attn",
    )(z3, z3, z3)


def _diff_kernel(q_ref, k_ref, v_ref, lam_ref, g_ref, o_ref, m_ref, l_ref, acc_ref, *, lam_init):
    qi = pl.program_id(2)
    tq = ATTN_TILE
    lane = lax.broadcasted_iota(jnp.int32, (tq, LANES), 1)
    lo = lane < HEAD_DIM
    q = q_ref[...]
    zero = jnp.zeros_like(q)
    qd = (jnp.where(lo, q, zero), jnp.where(lo, zero, q))

    k_own = _kv_tile(k_ref, qi)
    v_own = _kv_tile(v_ref, qi)
    for t in range(2):
        _softmax_step(t, _causal(_dot_nt(qd[t], k_own)), v_own, m_ref, l_ref, acc_ref, True)

    def body(j, carry):
        k = _kv_tile(k_ref, j)
        v = _kv_tile(v_ref, j)
        for t in range(2):
            _softmax_step(t, _dot_nt(qd[t], k), v, m_ref, l_ref, acc_ref, False)
        return carry

    lax.fori_loop(0, qi, body, 0)

    lv = lam_ref[...]
    lam = (jnp.exp(jnp.sum(lv[0:1] * lv[1:2], axis=1, keepdims=True))
           - jnp.exp(jnp.sum(lv[2:3] * lv[3:4], axis=1, keepdims=True)) + lam_init)
    o = _normalised(acc_ref, l_ref, 0) - lam * _normalised(acc_ref, l_ref, 1)
    o_ref[...] = (_rms(o, g_ref[...]) * (1.0 - lam_init)).astype(o_ref.dtype)


def _diff(z3, lam_rows, subln, lam_init):
    bsz, seq, _ = z3.shape
    tq = ATTN_TILE
    base = 3 * MOBA_HEADS * HEAD_DIM // LANES
    return pl.pallas_call(
        functools.partial(_diff_kernel, lam_init=lam_init),
        grid=(bsz, DIFF_HEADS, seq // tq),
        in_specs=[
            pl.BlockSpec((None, tq, LANES), lambda b, h, i: (b, i, base + h)),
            pl.BlockSpec((None, seq, LANES), lambda b, h, i: (b, 0, base + DIFF_HEADS + h)),
            pl.BlockSpec((None, seq, LANES), lambda b, h, i: (b, 0, base + 2 * DIFF_HEADS + h)),
            pl.BlockSpec((4, HEAD_DIM), lambda b, h, i: (0, 0)),
            pl.BlockSpec((1, LANES), lambda b, h, i: (0, 0)),
        ],
        out_specs=pl.BlockSpec((None, tq, LANES), lambda b, h, i: (b, i, h)),
        out_shape=jax.ShapeDtypeStruct((bsz, seq, DIFF_HEADS * LANES), _BF16),
        scratch_shapes=[
            pltpu.VMEM((2, tq, LANES), _F32),
            pltpu.VMEM((2, tq, LANES), _F32),
            pltpu.VMEM((2, tq, LANES), _F32),
        ],
        compiler_params=_params("parallel", "parallel", "arbitrary"),
        name="diff_attn",
    )(z3, z3, z3, lam_rows, subln)


def _mla_proj_kernel(x_ref, g_ref, w1_ref, qg_ref, kg_ref, wq_ref, wk_ref, wv_ref,
                     cq_ref, saq_ref, sbq_ref, ck_ref, sak_ref, sbk_ref,
                     q_out, k_out, v_out):
    h = _rms(x_ref[...], g_ref[...]).astype(_BF16)
    z = _dot(h, w1_ref[...])
    cq = _rms(z[:, :MLA_Q_RANK], qg_ref[...]).astype(_BF16)
    ckv = _rms(z[:, MLA_Q_RANK:MLA_Q_RANK + MLA_KV_RANK], kg_ref[...]).astype(_BF16)
    kr = z[:, MLA_Q_RANK + MLA_KV_RANK:]
    half = MLA_ROPE // 2
    kr = _rotate(kr, ck_ref[...], sak_ref[...], sbk_ref[...], half)
    qf = _dot(cq, wq_ref[...])
    kf = _dot(ckv, wk_ref[...])
    cqt, saq, sbq = cq_ref[...], saq_ref[...], sbq_ref[...]
    for hd in range(MLA_HEADS):
        sl = slice(hd * LANES, (hd + 1) * LANES)
        q_out[:, sl] = _rotate(qf[:, sl], cqt, saq, sbq, half).astype(_BF16)
        k_out[:, sl] = (kf[:, sl] + kr).astype(_BF16)
    v_out[...] = _dot(ckv, wv_ref[...]).astype(_BF16)


def _mla_proj(x2, g, w1, qg, kg, wq, wk, wv, tabs, seq):
    n = x2.shape[0]
    tm = TOKEN_TILE
    spb = seq // tm
    hw = MLA_HEADS * LANES
    tab_spec = pl.BlockSpec((tm, LANES), lambda i: (i % spb, 0))

    def full(a):
        return pl.BlockSpec(a.shape, lambda i: (0,) * a.ndim)

    return pl.pallas_call(
        _mla_proj_kernel,
        grid=(n // tm,),
        in_specs=[pl.BlockSpec((tm, D_MODEL), lambda i: (i, 0)),
                  full(g), full(w1), full(qg), full(kg), full(wq), full(wk), full(wv)]
                 + [tab_spec] * 6,
        out_specs=[pl.BlockSpec((tm, hw), lambda i: (i, 0)),
                   pl.BlockSpec((tm, hw), lambda i: (i, 0)),
                   pl.BlockSpec((tm, MLA_HEADS * MLA_V), lambda i: (i, 0))],
        out_shape=[jax.ShapeDtypeStruct((n, hw), _BF16),
                   jax.ShapeDtypeStruct((n, hw), _BF16),
                   jax.ShapeDtypeStruct((n, MLA_HEADS * MLA_V), _BF16)],
        compiler_params=_params("parallel"),
        name="mla_proj",
    )(x2, g, w1, qg, kg, wq, wk, wv, *tabs)


def _mla_kernel(q_ref, k_ref, v_ref, o_ref, m_ref, l_ref, acc_ref):
    qi = pl.program_id(2)
    tq = ATTN_TILE
    lane = lax.broadcasted_iota(jnp.int32, (tq, LANES), 1)
    lo = lane < MLA_V
    q = q_ref[...]
    qd = (q[:, :LANES], q[:, LANES:])

    k_own = _kv_tile(k_ref, qi)
    v_own = _kv_tile(v_ref, qi)
    for t in range(2):
        s = _dot_nt(qd[t], k_own[:, t * LANES:(t + 1) * LANES])
        _softmax_step(t, _causal(s), v_own, m_ref, l_ref, acc_ref, True)

    def body(j, carry):
        k = _kv_tile(k_ref, j)
        v = _kv_tile(v_ref, j)
        for t in range(2):
            s = _dot_nt(qd[t], k[:, t * LANES:(t + 1) * LANES])
            _softmax_step(t, s, v, m_ref, l_ref, acc_ref, False)
        return carry

    lax.fori_loop(0, qi, body, 0)
    o_ref[...] = jnp.where(lo, _normalised(acc_ref, l_ref, 0),
                           _normalised(acc_ref, l_ref, 1)).astype(o_ref.dtype)


def _mla(q3, k3, v3):
    bsz, seq, _ = q3.shape
    tq = ATTN_TILE
    pairs = MLA_HEADS // 2
    return pl.pallas_call(
        _mla_kernel,
        grid=(bsz, pairs, seq // tq),
        in_specs=[
            pl.BlockSpec((None, tq, 2 * LANES), lambda b, p, i: (b, i, p)),
            pl.BlockSpec((None, seq, 2 * LANES), lambda b, p, i: (b, 0, p)),
            pl.BlockSpec((None, seq, LANES), lambda b, p, i: (b, 0, p)),
        ],
        out_specs=pl.BlockSpec((None, tq, LANES), lambda b, p, i: (b, i, p)),
        out_shape=jax.ShapeDtypeStruct((bsz, seq, pairs * LANES), _BF16),
        scratch_shapes=[
            pltpu.VMEM((2, tq, LANES), _F32),
            pltpu.VMEM((2, tq, LANES), _F32),
            pltpu.VMEM((2, tq, LANES), _F32),
        ],
        compiler_params=_params("parallel", "parallel", "arbitrary"),
        name="mla_attn",
    )(q3, k3, v3)


def _route(logits):
    lane = lax.broadcasted_iota(jnp.int32, logits.shape, 1).astype(_F32)
    big = 4.0 * LANES
    is_group = (lane >= N_EXPERTS) & (lane < N_EXPERTS + N_GROUPS)
    gl = jnp.where(is_group, logits, -jnp.inf)
    gmax = jnp.max(gl, axis=1, keepdims=True)
    grp = jnp.min(jnp.where(gl == gmax, lane, big), axis=1, keepdims=True) - N_EXPERTS
    p_group = 1.0 / jnp.sum(jnp.exp(gl - gmax), axis=1, keepdims=True)
    in_group = (lane >= grp * EXPERTS_PER_GROUP) & (lane < (grp + 1) * EXPERTS_PER_GROUP)
    el = jnp.where(in_group, logits, -jnp.inf)
    e1 = jnp.max(el, axis=1, keepdims=True)
    i1 = jnp.min(jnp.where(el == e1, lane, big), axis=1, keepdims=True)
    el2 = jnp.where(lane == i1, -jnp.inf, el)
    e2 = jnp.max(el2, axis=1, keepdims=True)
    i2 = jnp.min(jnp.where(el2 == e2, lane, big), axis=1, keepdims=True)
    r = jnp.exp(e2 - e1)
    w1 = p_group / (1.0 + r)
    w2 = w1 * r
    return jnp.where(lane == i1, w1, 0.0) + jnp.where(lane == i2, w2, 0.0)


def _out_proj_kernel(*refs, n_mix):
    x_ref = refs[0]
    mix = refs[1:1 + 2 * n_mix]
    g_ref, wr_ref, br_ref = refs[1 + 2 * n_mix:4 + 2 * n_mix]
    x_out, t_out, gate_out = refs[4 + 2 * n_mix:]
    x = x_ref[...]
    for i in range(n_mix):
        x = x + _dot(mix[2 * i][...], mix[2 * i + 1][...])
    x_out[...] = x
    t = _rms(x, g_ref[...]).astype(_BF16)
    t_out[...] = t
    gate_out[...] = _route(_dot(t, wr_ref[...]) + br_ref[...])


def _out_proj(x2, mixes, g, wr, br):
    n = x2.shape[0]
    tm = TOKEN_TILE
    in_specs = [pl.BlockSpec((tm, D_MODEL), lambda i: (i, 0))]
    args = [x2]
    for o, w in mixes:
        in_specs.append(pl.BlockSpec((tm, o.shape[1]), lambda i: (i, 0)))
        in_specs.append(pl.BlockSpec(w.shape, lambda i: (0, 0)))
        args += [o, w]
    for a in (g, wr, br):
        in_specs.append(pl.BlockSpec(a.shape, lambda i: (0, 0)))
        args.append(a)
    return pl.pallas_call(
        functools.partial(_out_proj_kernel, n_mix=len(mixes)),
        grid=(n // tm,),
        in_specs=in_specs,
        out_specs=[pl.BlockSpec((tm, D_MODEL), lambda i: (i, 0)),
                   pl.BlockSpec((tm, D_MODEL), lambda i: (i, 0)),
                   pl.BlockSpec((tm, LANES), lambda i: (i, 0))],
        out_shape=[jax.ShapeDtypeStruct((n, D_MODEL), _F32),
                   jax.ShapeDtypeStruct((n, D_MODEL), _BF16),
                   jax.ShapeDtypeStruct((n, LANES), _F32)],
        compiler_params=_params("parallel"),
        name="out_proj_route",
    )(*args)


def _moe_kernel(x_ref, t_ref, gate_ref, wg_ref, wu_ref, wd_ref, fg_ref, o_ref, acc_ref, *, final):
    e = pl.program_id(1)

    @pl.when(e == 0)
    def _():
        acc_ref[...] = x_ref[...]

    t = t_ref[...]
    gates = gate_ref[...]
    lane = lax.broadcasted_iota(jnp.int32, gates.shape, 1)
    ge = jnp.sum(jnp.where(lane == e, gates, 0.0), axis=1, keepdims=True)
    hg = _dot(t, wg_ref[...])
    hu = _dot(t, wu_ref[...])
    act = hg * (1.0 / (1.0 + jnp.exp(-hg))) * hu * ge
    acc_ref[...] += _dot(act.astype(_BF16), wd_ref[...])

    @pl.when(e == pl.num_programs(1) - 1)
    def _():
        y = acc_ref[...]
        if final:
            y = _rms(y, fg_ref[...])
        o_ref[...] = y


def _moe(x2, t, gates, wg, wu, wd, fg, final):
    n = x2.shape[0]
    tm = min(MOE_TILE, n)
    return pl.pallas_call(
        functools.partial(_moe_kernel, final=final),
        grid=(n // tm, N_EXPERTS),
        in_specs=[
            pl.BlockSpec((tm, D_MODEL), lambda i, e: (i, 0)),
            pl.BlockSpec((tm, D_MODEL), lambda i, e: (i, 0)),
            pl.BlockSpec((tm, LANES), lambda i, e: (i, 0)),
            pl.BlockSpec((None, D_MODEL, EXPERT_FF), lambda i, e: (e, 0, 0)),
            pl.BlockSpec((None, D_MODEL, EXPERT_FF), lambda i, e: (e, 0, 0)),
            pl.BlockSpec((None, EXPERT_FF, D_MODEL), lambda i, e: (e, 0, 0)),
            pl.BlockSpec((1, D_MODEL), lambda i, e: (0, 0)),
        ],
        out_specs=pl.BlockSpec((tm, D_MODEL), lambda i, e: (i, 0)),
        out_shape=jax.ShapeDtypeStruct((n, D_MODEL), _F32),
        scratch_shapes=[pltpu.VMEM((tm, D_MODEL), _F32)],
        compiler_params=_params("parallel", "arbitrary"),
        name="moe_experts",
    )(x2, t, gates, wg, wu, wd, fg)


def _router_weights(w_group, b_group, w_expert, b_expert):
    pad = LANES - N_EXPERTS - N_GROUPS
    wr = jnp.concatenate([w_expert, w_group, jnp.zeros((D_MODEL, pad), _F32)], axis=1)
    br = jnp.concatenate([b_expert, b_group, jnp.zeros((pad,), _F32)])[None, :]
    return wr.astype(_BF16), br


def _mla_weights(w_in, w_q_up, w_kv_up):
    lat = MLA_Q_RANK + MLA_KV_RANK
    w1 = jnp.concatenate([w_in[:, :lat], jnp.zeros((D_MODEL, MLA_NOPE), _F32), w_in[:, lat:],
                          jnp.zeros((D_MODEL, LANES - MLA_NOPE - MLA_ROPE), _F32)], axis=1)
    qd = MLA_NOPE + MLA_ROPE
    wq = jnp.pad(w_q_up.reshape(MLA_Q_RANK, MLA_HEADS, qd), ((0, 0), (0, 0), (0, LANES - qd)))
    kv = w_kv_up.reshape(MLA_KV_RANK, MLA_HEADS, MLA_NOPE + MLA_V)
    wk = jnp.pad(kv[:, :, :MLA_NOPE], ((0, 0), (0, 0), (0, LANES - MLA_NOPE)))
    wv = kv[:, :, MLA_NOPE:]
    return (w1.astype(_BF16), wq.reshape(MLA_Q_RANK, -1).astype(_BF16),
            wk.reshape(MLA_KV_RANK, -1).astype(_BF16), wv.reshape(MLA_KV_RANK, -1).astype(_BF16))


def kernel(x, attn_norm, ev_w_in, ev_lambda_q1, ev_lambda_k1, ev_lambda_q2, ev_lambda_k2, ev_subln, ev_w_out, od_w_in, od_q_norm, od_kv_norm, od_w_q_up, od_w_kv_up, od_w_out, ffn_norm, moe_w_group, moe_b_group, moe_w_expert, moe_b_expert, moe_w_gate, moe_w_up, moe_w_down, final_norm):
    bsz, seq, d = x.shape
    n = bsz * seq
    x2 = x.reshape(n, d)
    fg = final_norm[None, :]

    lam_init = 0.8 - 0.6 * math.exp(-0.3 * 0)
    z = _even_proj(x2, attn_norm[0][None, :], ev_w_in[0].astype(_BF16), _even_rope_tables(seq), seq)
    z3 = z.reshape(bsz, seq, EVEN_IN)
    o_a = _moba(z3).reshape(n, -1)
    lam_rows = jnp.stack([ev_lambda_q1[0], ev_lambda_k1[0], ev_lambda_q2[0], ev_lambda_k2[0]])
    o_b = _diff(z3, lam_rows, ev_subln[0][None, :], lam_init).reshape(n, -1)
    w_out = ev_w_out[0].astype(_BF16)
    wa, wb = w_out[:o_a.shape[1]], w_out[o_a.shape[1]:]
    wr, br = _router_weights(moe_w_group[0], moe_b_group[0], moe_w_expert[0], moe_b_expert[0])
    x2, t, gates = _out_proj(x2, [(o_a, wa), (o_b, wb)], ffn_norm[0][None, :], wr, br)
    x2 = _moe(x2, t, gates, moe_w_gate[0].astype(_BF16), moe_w_up[0].astype(_BF16),
              moe_w_down[0].astype(_BF16), fg, False)

    w1, wq, wk, wv = _mla_weights(od_w_in[0], od_w_q_up[0], od_w_kv_up[0])
    tabs = _mla_rope_tables(seq, (MLA_NOPE + MLA_ROPE) ** -0.5)
    q, k, v = _mla_proj(x2, attn_norm[1][None, :], w1, od_q_norm[0][None, :],
                        od_kv_norm[0][None, :], wq, wk, wv, tabs, seq)
    o_c = _mla(q.reshape(bsz, seq, -1), k.reshape(bsz, seq, -1), v.reshape(bsz, seq, -1))
    wr, br = _router_weights(moe_w_group[1], moe_b_group[1], moe_w_expert[1], moe_b_expert[1])
    x2, t, gates = _out_proj(x2, [(o_c.reshape(n, -1), od_w_out[0].astype(_BF16))],
                             ffn_norm[1][None, :], wr, br)
    out = _moe(x2, t, gates, moe_w_gate[1].astype(_BF16), moe_w_up[1].astype(_BF16),
               moe_w_down[1].astype(_BF16), fg, True)
    return out.reshape(bsz, seq, d)
```

```python
import functools
import math

import jax
import jax.numpy as jnp
from jax import lax
from jax.experimental import pallas as pl
from jax.experimental.pallas import tpu as pltpu

D_MODEL = 1024
ROPE_THETA = 10000.0
NORM_EPS = 1e-6

HEAD_DIM = 64
MOBA_HEADS = 8
MOBA_BLOCK = 256
MOBA_TOPK = 3
DIFF_HEADS = 4
EVEN_IN = 3072

MLA_HEADS = 16
MLA_Q_RANK = 256
MLA_KV_RANK = 128
MLA_NOPE = 64
MLA_ROPE = 32
MLA_V = 64

N_GROUPS = 4
EXPERTS_PER_GROUP = 8
N_EXPERTS = 32
EXPERT_FF = 256

LANES = 128
LOG2E = math.log2(math.e)
MASK_BIAS = -1e30
VMEM_LIMIT = 48 * 1024 * 1024

TOKEN_TILE = 512
MOE_TILE = 1024
Q_TILE = 256
KV_TILE = 1024

_F32 = jnp.float32
_BF16 = jnp.bfloat16


def _dot(a, b):
    return jnp.dot(a, b, preferred_element_type=_F32)


def _rms(x, g):
    return x * lax.rsqrt(jnp.mean(x * x, axis=-1, keepdims=True) + NORM_EPS) * g


def _params(*sem):
    return pltpu.CompilerParams(dimension_semantics=sem, vmem_limit_bytes=VMEM_LIMIT)


def _rope_tables(seq, dim):
    inv = 1.0 / (ROPE_THETA ** (jnp.arange(0, dim, 2, dtype=_F32) / dim))
    ang = jnp.arange(seq, dtype=_F32)[:, None] * inv[None, :]
    return jnp.cos(ang), jnp.sin(ang)


def _even_rope_tables(seq):
    cos, sin = _rope_tables(seq, HEAD_DIM)
    half = HEAD_DIM // 2
    lane = jnp.arange(LANES)
    first = (lane % HEAD_DIM) < half
    c = cos[:, lane % half]
    s = sin[:, lane % half]
    return c, jnp.where(first, -s, 0.0), jnp.where(first, 0.0, s)


def _mla_rope_tables(seq, scale):
    cos, sin = _rope_tables(seq, MLA_ROPE)
    half = MLA_ROPE // 2
    lane = jnp.arange(LANES)
    r = (lane - MLA_NOPE) % half
    in_rope = (lane >= MLA_NOPE) & (lane < MLA_NOPE + MLA_ROPE)
    first = (lane >= MLA_NOPE) & (lane < MLA_NOPE + half)
    second = (lane >= MLA_NOPE + half) & (lane < MLA_NOPE + MLA_ROPE)
    c, s = cos[:, r], sin[:, r]
    cq = jnp.where(lane < MLA_NOPE, 1.0, jnp.where(in_rope, c, 0.0)) * scale
    ck = jnp.where(in_rope, c, 0.0)
    sa = jnp.where(first, -s, 0.0)
    sb = jnp.where(second, s, 0.0)
    return cq, sa * scale, sb * scale, ck, sa, sb


def _rotate(z, c, sa, sb, half):
    return z * c + pltpu.roll(z, LANES - half, 1) * sa + pltpu.roll(z, half, 1) * sb


def _store_transposed(vt_ref, row0, z):
    vt_ref[row0:row0 + LANES, :] = z.T.astype(_BF16)


def _vt_spec(features, tm):
    per_slab = KV_TILE // tm
    return pl.BlockSpec((None, features, tm), lambda i: (i // per_slab, 0, i % per_slab))


def _even_proj_kernel(x_ref, g_ref, w_ref, c_ref, sa_ref, sb_ref, qk_ref, vt_ref):
    h = _rms(x_ref[...], g_ref[...]).astype(_BF16)
    c, sa, sb = c_ref[...], sa_ref[...], sb_ref[...]
    chunk = 512
    blocks = chunk // LANES
    qk_chunk = {0: 0, 1: 1, 3: 2, 4: 3}
    v_chunk = {2: 0, 5: 1}
    for ci in range(EVEN_IN // chunk):
        z = _dot(h, w_ref[:, ci * chunk:(ci + 1) * chunk])
        for b in range(blocks):
            zz = z[:, b * LANES:(b + 1) * LANES]
            if ci in v_chunk:
                _store_transposed(vt_ref, (v_chunk[ci] * blocks + b) * LANES, zz)
                continue
            zz = _rotate(zz, c, sa, sb, HEAD_DIM // 2)
            if ci in (0, 3):
                zz = zz * (HEAD_DIM ** -0.5 * LOG2E)
            col = (qk_chunk[ci] * blocks + b) * LANES
            qk_ref[:, col:col + LANES] = zz.astype(_BF16)


def _even_proj(x2, g, w, tabs, seq):
    n = x2.shape[0]
    tm = TOKEN_TILE
    spb = seq // tm
    tab_spec = pl.BlockSpec((tm, LANES), lambda i: (i % spb, 0))
    qk_w = 4 * 512
    v_w = 2 * 512
    return pl.pallas_call(
        _even_proj_kernel,
        grid=(n // tm,),
        in_specs=[
            pl.BlockSpec((tm, D_MODEL), lambda i: (i, 0)),
            pl.BlockSpec((1, D_MODEL), lambda i: (0, 0)),
            pl.BlockSpec((D_MODEL, EVEN_IN), lambda i: (0, 0)),
            tab_spec, tab_spec, tab_spec,
        ],
        out_specs=[pl.BlockSpec((tm, qk_w), lambda i: (i, 0)), _vt_spec(v_w, tm)],
        out_shape=[jax.ShapeDtypeStruct((n, qk_w), _BF16),
                   jax.ShapeDtypeStruct((n // KV_TILE, v_w, KV_TILE), _BF16)],
        compiler_params=_params("parallel"),
        name="even_proj",
    )(x2, g, w, *tabs)


def _attn_step(t, s, vt, m_ref, l_ref, acc_ref, first):
    tile_max = jnp.max(s, axis=0, keepdims=True)
    if first:
        m_new = tile_max
        p = jnp.exp2(s - m_new)
        l_new = jnp.sum(p, axis=0, keepdims=True)
        acc = _dot(vt, p.astype(_BF16))
    else:
        m_prev = m_ref[t]
        m_new = jnp.maximum(m_prev, tile_max)
        alpha = jnp.exp2(m_prev - m_new)
        p = jnp.exp2(s - m_new)
        l_new = alpha * l_ref[t] + jnp.sum(p, axis=0, keepdims=True)
        acc = alpha * acc_ref[t] + _dot(vt, p.astype(_BF16))
    m_ref[t] = m_new
    l_ref[t] = l_new
    acc_ref[t] = acc


def _causal(s, key0, qry0):
    key = lax.broadcasted_iota(jnp.int32, s.shape, 0) + key0
    qry = lax.broadcasted_iota(jnp.int32, s.shape, 1) + qry0
    return jnp.where(key <= qry, s, -jnp.inf)


def _kv_rows(c):
    return pl.ds(pl.multiple_of(c * KV_TILE, KV_TILE), KV_TILE)


def _transposed(q):
    return q.astype(_F32).T


def _attn_scratch(dv):
    return [pltpu.VMEM((2, 1, Q_TILE), _F32),
            pltpu.VMEM((2, 1, Q_TILE), _F32),
            pltpu.VMEM((2, dv, Q_TILE), _F32)]


def _sweep(qi, step):
    last = (qi * Q_TILE) // KV_TILE
    step(last, True)

    def body(c, carry):
        step(c, False)
        return carry

    lax.fori_loop(0, last, body, 0)


def _moba_kernel(q_ref, k_ref, vt_ref, o_ref, km_ref, ka_ref, m_ref, l_ref, acc_ref, *, nb, topk):
    qi = pl.program_id(2)
    tq = Q_TILE
    half = HEAD_DIM // 2

    @pl.when(qi == 0)
    def _():
        kf = k_ref[...].astype(_F32)
        means = jnp.sum(kf.reshape(nb, MOBA_BLOCK, LANES), axis=1) * (1.0 / MOBA_BLOCK)
        lane_b = lax.broadcasted_iota(jnp.int32, (nb, LANES), 1)
        km_ref[...] = jnp.zeros(km_ref.shape, _F32)
        km_ref[0:nb, :] = jnp.where(lane_b < HEAD_DIM, 0.0, means)
        km_ref[HEAD_DIM:HEAD_DIM + nb, :] = jnp.where(lane_b < HEAD_DIM, means, 0.0)

        lane = lax.broadcasted_iota(jnp.int32, (KV_TILE, LANES), 1)
        blk = lax.broadcasted_iota(jnp.int32, (KV_TILE, LANES), 0) // MOBA_BLOCK

        def build(c, carry):
            rows = _kv_rows(c)
            k = k_ref[rows, :].astype(_F32)
            b = blk + c * (KV_TILE // MOBA_BLOCK)
            ka_ref[0, rows, :] = jnp.where(lane < HEAD_DIM, k,
                                           jnp.where(lane == b + HEAD_DIM, 1.0, 0.0)).astype(_BF16)
            ka_ref[1, rows, :] = jnp.where(lane < HEAD_DIM,
                                           jnp.where(lane == b, 1.0, 0.0), k).astype(_BF16)
            return carry

        lax.fori_loop(0, k_ref.shape[0] // KV_TILE, build, 0)

    qt = _transposed(q_ref[...])
    gate = _dot(km_ref[...].astype(_BF16), qt.astype(_BF16))
    row = lax.broadcasted_iota(jnp.int32, (half, tq), 0).astype(_F32)
    own = qi.astype(_F32)

    def block_bias(base):
        g = jnp.where(row < own, gate[base:base + half], -jnp.inf)
        bias = jnp.where(row == own, 0.0, MASK_BIAS)
        for _ in range(topk):
            mx = jnp.max(g, axis=0, keepdims=True)
            hit = (g == mx) & (mx > -jnp.inf)
            idx = jnp.min(jnp.where(hit, row, 4.0 * LANES), axis=0, keepdims=True)
            pick = row == idx
            bias = jnp.where(pick, 0.0, bias)
            g = jnp.where(pick, -jnp.inf, g)
        return bias

    z_half = jnp.zeros((half, tq), _F32)
    qs = (jnp.concatenate([qt[:HEAD_DIM], block_bias(HEAD_DIM), z_half], axis=0).astype(_BF16),
          jnp.concatenate([block_bias(0), z_half, qt[HEAD_DIM:]], axis=0).astype(_BF16))

    def step(c, diagonal):
        vt = vt_ref[c]
        for t in range(2):
            s = _dot(ka_ref[t, _kv_rows(c), :], qs[t])
            if diagonal:
                s = _causal(s, c * KV_TILE, qi * tq)
            _attn_step(t, s, vt[t * HEAD_DIM:(t + 1) * HEAD_DIM], m_ref, l_ref, acc_ref, diagonal)

    _sweep(qi, step)
    ot = jnp.concatenate([acc_ref[0] / l_ref[0], acc_ref[1] / l_ref[1]], axis=0)
    o_ref[...] = ot.T.astype(o_ref.dtype)


def _moba(qk3, vt):
    bsz, seq, _ = qk3.shape
    tq = Q_TILE
    assert MOBA_BLOCK == tq and seq % KV_TILE == 0
    nb = seq // MOBA_BLOCK
    assert nb <= HEAD_DIM // 2
    topk = max(1, min(MOBA_TOPK, nb - 1))
    pairs = MOBA_HEADS // 2
    return pl.pallas_call(
        functools.partial(_moba_kernel, nb=nb, topk=topk),
        grid=(bsz, pairs, seq // tq),
        in_specs=[
            pl.BlockSpec((None, tq, LANES), lambda b, p, i: (b, i, p)),
            pl.BlockSpec((None, seq, LANES), lambda b, p, i: (b, 0, pairs + p)),
            pl.BlockSpec((seq // KV_TILE, LANES, KV_TILE), lambda b, p, i: (b, p, 0)),
        ],
        out_specs=pl.BlockSpec((None, tq, LANES), lambda b, p, i: (b, i, p)),
        out_shape=jax.ShapeDtypeStruct((bsz, seq, pairs * LANES), _BF16),
        scratch_shapes=[pltpu.VMEM((LANES, LANES), _F32), pltpu.VMEM((2, seq, LANES), _BF16)]
                       + _attn_scratch(HEAD_DIM),
        compiler_params=_params("parallel", "parallel", "arbitrary"),
        name="moba_attn",
    )(qk3, qk3, vt)


def _diff_kernel(q_ref, k_ref, vt_ref, lam_ref, g_ref, o_ref, m_ref, l_ref, acc_ref, *, lam_init):
    qi = pl.program_id(2)
    tq = Q_TILE
    qt = _transposed(q_ref[...])
    z_head = jnp.zeros((HEAD_DIM, tq), _F32)
    qd = (jnp.concatenate([qt[:HEAD_DIM], z_head], axis=0).astype(_BF16),
          jnp.concatenate([z_head, qt[HEAD_DIM:]], axis=0).astype(_BF16))

    def step(c, diagonal):
        k = k_ref[_kv_rows(c), :]
        vt = vt_ref[c]
        for t in range(2):
            s = _dot(k, qd[t])
            if diagonal:
                s = _causal(s, c * KV_TILE, qi * tq)
            _attn_step(t, s, vt, m_ref, l_ref, acc_ref, diagonal)

    _sweep(qi, step)

    lv = lam_ref[...]
    lam = (jnp.exp(jnp.sum(lv[0:1] * lv[1:2], axis=1, keepdims=True))
           - jnp.exp(jnp.sum(lv[2:3] * lv[3:4], axis=1, keepdims=True)) + lam_init)
    ot = acc_ref[0] / l_ref[0] - lam * (acc_ref[1] / l_ref[1])
    o_ref[...] = (_rms(ot.T, g_ref[...]) * (1.0 - lam_init)).astype(o_ref.dtype)


def _diff(qk3, vt, lam_rows, subln, lam_init):
    bsz, seq, _ = qk3.shape
    tq = Q_TILE
    base = 2 * MOBA_HEADS * HEAD_DIM // LANES
    vbase = MOBA_HEADS * HEAD_DIM // LANES
    return pl.pallas_call(
        functools.partial(_diff_kernel, lam_init=lam_init),
        grid=(bsz, DIFF_HEADS, seq // tq),
        in_specs=[
            pl.BlockSpec((None, tq, LANES), lambda b, h, i: (b, i, base + h)),
            pl.BlockSpec((None, seq, LANES), lambda b, h, i: (b, 0, base + DIFF_HEADS + h)),
            pl.BlockSpec((seq // KV_TILE, LANES, KV_TILE), lambda b, h, i: (b, vbase + h, 0)),
            pl.BlockSpec((4, HEAD_DIM), lambda b, h, i: (0, 0)),
            pl.BlockSpec((1, LANES), lambda b, h, i: (0, 0)),
        ],
        out_specs=pl.BlockSpec((None, tq, LANES), lambda b, h, i: (b, i, h)),
        out_shape=jax.ShapeDtypeStruct((bsz, seq, DIFF_HEADS * LANES), _BF16),
        scratch_shapes=_attn_scratch(LANES),
        compiler_params=_params("parallel", "parallel", "arbitrary"),
        name="diff_attn",
    )(qk3, qk3, vt, lam_rows, subln)


def _mla_proj_kernel(x_ref, g_ref, w1_ref, qg_ref, kg_ref, wq_ref, wk_ref, wv_ref,
                     cq_ref, saq_ref, sbq_ref, ck_ref, sak_ref, sbk_ref,
                     q_out, k_out, vt_out):
    h = _rms(x_ref[...], g_ref[...]).astype(_BF16)
    z = _dot(h, w1_ref[...])
    cq = _rms(z[:, :MLA_Q_RANK], qg_ref[...]).astype(_BF16)
    ckv = _rms(z[:, MLA_Q_RANK:MLA_Q_RANK + MLA_KV_RANK], kg_ref[...]).astype(_BF16)
    kr = z[:, MLA_Q_RANK + MLA_KV_RANK:]
    half = MLA_ROPE // 2
    kr = _rotate(kr, ck_ref[...], sak_ref[...], sbk_ref[...], half)
    qf = _dot(cq, wq_ref[...])
    kf = _dot(ckv, wk_ref[...])
    cqt, saq, sbq = cq_ref[...], saq_ref[...], sbq_ref[...]
    for hd in range(MLA_HEADS):
        sl = slice(hd * LANES, (hd + 1) * LANES)
        q_out[:, sl] = _rotate(qf[:, sl], cqt, saq, sbq, half).astype(_BF16)
        k_out[:, sl] = (kf[:, sl] + kr).astype(_BF16)
    vf = _dot(ckv, wv_ref[...])
    for b in range(MLA_HEADS * MLA_V // LANES):
        _store_transposed(vt_out, b * LANES, vf[:, b * LANES:(b + 1) * LANES])


def _mla_proj(x2, g, w1, qg, kg, wq, wk, wv, tabs, seq):
    n = x2.shape[0]
    tm = TOKEN_TILE
    spb = seq // tm
    hw = MLA_HEADS * LANES
    vw = MLA_HEADS * MLA_V
    tab_spec = pl.BlockSpec((tm, LANES), lambda i: (i % spb, 0))

    def full(a):
        return pl.BlockSpec(a.shape, lambda i: (0,) * a.ndim)

    return pl.pallas_call(
        _mla_proj_kernel,
        grid=(n // tm,),
        in_specs=[pl.BlockSpec((tm, D_MODEL), lambda i: (i, 0)),
                  full(g), full(w1), full(qg), full(kg), full(wq), full(wk), full(wv)]
                 + [tab_spec] * 6,
        out_specs=[pl.BlockSpec((tm, hw), lambda i: (i, 0)),
                   pl.BlockSpec((tm, hw), lambda i: (i, 0)),
                   _vt_spec(vw, tm)],
        out_shape=[jax.ShapeDtypeStruct((n, hw), _BF16),
                   jax.ShapeDtypeStruct((n, hw), _BF16),
                   jax.ShapeDtypeStruct((n // KV_TILE, vw, KV_TILE), _BF16)],
        compiler_params=_params("parallel"),
        name="mla_proj",
    )(x2, g, w1, qg, kg, wq, wk, wv, *tabs)


def _mla_kernel(q_ref, k_ref, vt_ref, o_ref, m_ref, l_ref, acc_ref):
    qi = pl.program_id(2)
    q = q_ref[...]
    qd = tuple(_transposed(q[:, t * LANES:(t + 1) * LANES]).astype(_BF16) for t in range(2))

    def step(c, diagonal):
        vt = vt_ref[c]
        for t in range(2):
            s = _dot(k_ref[_kv_rows(c), t * LANES:(t + 1) * LANES], qd[t])
            if diagonal:
                s = _causal(s, c * KV_TILE, qi * Q_TILE)
            _attn_step(t, s, vt[t * MLA_V:(t + 1) * MLA_V], m_ref, l_ref, acc_ref, diagonal)

    _sweep(qi, step)
    ot = jnp.concatenate([acc_ref[0] / l_ref[0], acc_ref[1] / l_ref[1]], axis=0)
    o_ref[...] = ot.T.astype(o_ref.dtype)


def _mla(q3, k3, vt):
    bsz, seq, _ = q3.shape
    tq = Q_TILE
    pairs = MLA_HEADS // 2
    return pl.pallas_call(
        _mla_kernel,
        grid=(bsz, pairs, seq // tq),
        in_specs=[
            pl.BlockSpec((None, tq, 2 * LANES), lambda b, p, i: (b, i, p)),
            pl.BlockSpec((None, seq, 2 * LANES), lambda b, p, i: (b, 0, p)),
            pl.BlockSpec((seq // KV_TILE, LANES, KV_TILE), lambda b, p, i: (b, p, 0)),
        ],
        out_specs=pl.BlockSpec((None, tq, LANES), lambda b, p, i: (b, i, p)),
        out_shape=jax.ShapeDtypeStruct((bsz, seq, pairs * LANES), _BF16),
        scratch_shapes=_attn_scratch(MLA_V),
        compiler_params=_params("parallel", "parallel", "arbitrary"),
        name="mla_attn",
    )(q3, k3, vt)


def _route(logits):
    lane = lax.broadcasted_iota(jnp.int32, logits.shape, 1).astype(_F32)
    big = 4.0 * LANES
    is_group = (lane >= N_EXPERTS) & (lane < N_EXPERTS + N_GROUPS)
    gl = jnp.where(is_group, logits, -jnp.inf)
    gmax = jnp.max(gl, axis=1, keepdims=True)
    grp = jnp.min(jnp.where(gl == gmax, lane, big), axis=1, keepdims=True) - N_EXPERTS
    p_group = 1.0 / jnp.sum(jnp.exp(gl - gmax), axis=1, keepdims=True)
    in_group = (lane >= grp * EXPERTS_PER_GROUP) & (lane < (grp + 1) * EXPERTS_PER_GROUP)
    el = jnp.where(in_group, logits, -jnp.inf)
    e1 = jnp.max(el, axis=1, keepdims=True)
    i1 = jnp.min(jnp.where(el == e1, lane, big), axis=1, keepdims=True)
    el2 = jnp.where(lane == i1, -jnp.inf, el)
    e2 = jnp.max(el2, axis=1, keepdims=True)
    i2 = jnp.min(jnp.where(el2 == e2, lane, big), axis=1, keepdims=True)
    r = jnp.exp(e2 - e1)
    w1 = p_group / (1.0 + r)
    w2 = w1 * r
    return jnp.where(lane == i1, w1, 0.0) + jnp.where(lane == i2, w2, 0.0)


def _out_proj_kernel(*refs, n_mix):
    x_ref = refs[0]
    mix = refs[1:1 + 2 * n_mix]
    g_ref, wr_ref, br_ref = refs[1 + 2 * n_mix:4 + 2 * n_mix]
    x_out, t_out, gate_out = refs[4 + 2 * n_mix:]
    x = x_ref[...]
    for i in range(n_mix):
        x = x + _dot(mix[2 * i][...], mix[2 * i + 1][...])
    x_out[...] = x
    t = _rms(x, g_ref[...]).astype(_BF16)
    t_out[...] = t
    gate_out[...] = _route(_dot(t, wr_ref[...]) + br_ref[...])


def _out_proj(x2, mixes, g, wr, br):
    n = x2.shape[0]
    tm = TOKEN_TILE
    in_specs = [pl.BlockSpec((tm, D_MODEL), lambda i: (i, 0))]
    args = [x2]
    for o, w in mixes:
        in_specs.append(pl.BlockSpec((tm, o.shape[1]), lambda i: (i, 0)))
        in_specs.append(pl.BlockSpec(w.shape, lambda i: (0, 0)))
        args += [o, w]
    for a in (g, wr, br):
        in_specs.append(pl.BlockSpec(a.shape, lambda i: (0, 0)))
        args.append(a)
    return pl.pallas_call(
        functools.partial(_out_proj_kernel, n_mix=len(mixes)),
        grid=(n // tm,),
        in_specs=in_specs,
        out_specs=[pl.BlockSpec((tm, D_MODEL), lambda i: (i, 0)),
                   pl.BlockSpec((tm, D_MODEL), lambda i: (i, 0)),
                   pl.BlockSpec((tm, LANES), lambda i: (i, 0))],
        out_shape=[jax.ShapeDtypeStruct((n, D_MODEL), _F32),
                   jax.ShapeDtypeStruct((n, D_MODEL), _BF16),
                   jax.ShapeDtypeStruct((n, LANES), _F32)],
        compiler_params=_params("parallel"),
        name="out_proj_route",
    )(*args)


def _moe_kernel(x_ref, t_ref, gate_ref, wg_ref, wu_ref, wd_ref, fg_ref, o_ref, acc_ref, *, final):
    e = pl.program_id(1)

    @pl.when(e == 0)
    def _():
        acc_ref[...] = x_ref[...]

    t = t_ref[...]
    gates = gate_ref[...]
    lane = lax.broadcasted_iota(jnp.int32, gates.shape, 1)
    ge = jnp.sum(jnp.where(lane == e, gates, 0.0), axis=1, keepdims=True)
    hg = _dot(t, wg_ref[...])
    hu = _dot(t, wu_ref[...])
    act = hg * (1.0 / (1.0 + jnp.exp(-hg))) * hu * ge
    acc_ref[...] += _dot(act.astype(_BF16), wd_ref[...])

    @pl.when(e == pl.num_programs(1) - 1)
    def _():
        y = acc_ref[...]
        if final:
            y = _rms(y, fg_ref[...])
        o_ref[...] = y


def _moe(x2, t, gates, wg, wu, wd, fg, final):
    n = x2.shape[0]
    tm = min(MOE_TILE, n)
    return pl.pallas_call(
        functools.partial(_moe_kernel, final=final),
        grid=(n // tm, N_EXPERTS),
        in_specs=[
            pl.BlockSpec((tm, D_MODEL), lambda i, e: (i, 0)),
            pl.BlockSpec((tm, D_MODEL), lambda i, e: (i, 0)),
            pl.BlockSpec((tm, LANES), lambda i, e: (i, 0)),
            pl.BlockSpec((None, D_MODEL, EXPERT_FF), lambda i, e: (e, 0, 0)),
            pl.BlockSpec((None, D_MODEL, EXPERT_FF), lambda i, e: (e, 0, 0)),
            pl.BlockSpec((None, EXPERT_FF, D_MODEL), lambda i, e: (e, 0, 0)),
            pl.BlockSpec((1, D_MODEL), lambda i, e: (0, 0)),
        ],
        out_specs=pl.BlockSpec((tm, D_MODEL), lambda i, e: (i, 0)),
        out_shape=jax.ShapeDtypeStruct((n, D_MODEL), _F32),
        scratch_shapes=[pltpu.VMEM((tm, D_MODEL), _F32)],
        compiler_params=_params("parallel", "arbitrary"),
        name="moe_experts",
    )(x2, t, gates, wg, wu, wd, fg)


def _router_weights(w_group, b_group, w_expert, b_expert):
    pad = LANES - N_EXPERTS - N_GROUPS
    wr = jnp.concatenate([w_expert, w_group, jnp.zeros((D_MODEL, pad), _F32)], axis=1)
    br = jnp.concatenate([b_expert, b_group, jnp.zeros((pad,), _F32)])[None, :]
    return wr.astype(_BF16), br


def _mla_weights(w_in, w_q_up, w_kv_up):
    lat = MLA_Q_RANK + MLA_KV_RANK
    w1 = jnp.concatenate([w_in[:, :lat], jnp.zeros((D_MODEL, MLA_NOPE), _F32), w_in[:, lat:],
                          jnp.zeros((D_MODEL, LANES - MLA_NOPE - MLA_ROPE), _F32)], axis=1)
    qd = MLA_NOPE + MLA_ROPE
    wq = jnp.pad(w_q_up.reshape(MLA_Q_RANK, MLA_HEADS, qd), ((0, 0), (0, 0), (0, LANES - qd)))
    kv = w_kv_up.reshape(MLA_KV_RANK, MLA_HEADS, MLA_NOPE + MLA_V)
    wk = jnp.pad(kv[:, :, :MLA_NOPE], ((0, 0), (0, 0), (0, LANES - MLA_NOPE)))
    wv = kv[:, :, MLA_NOPE:]
    return (w1.astype(_BF16), wq.reshape(MLA_Q_RANK, -1).astype(_BF16),
            wk.reshape(MLA_KV_RANK, -1).astype(_BF16), wv.reshape(MLA_KV_RANK, -1).astype(_BF16))


def kernel(x, attn_norm, ev_w_in, ev_lambda_q1, ev_lambda_k1, ev_lambda_q2, ev_lambda_k2, ev_subln, ev_w_out, od_w_in, od_q_norm, od_kv_norm, od_w_q_up, od_w_kv_up, od_w_out, ffn_norm, moe_w_group, moe_b_group, moe_w_expert, moe_b_expert, moe_w_gate, moe_w_up, moe_w_down, final_norm):
    bsz, seq, d = x.shape
    n = bsz * seq
    x2 = x.reshape(n, d)
    fg = final_norm[None, :]

    lam_init = 0.8 - 0.6 * math.exp(-0.3 * 0)
    qk, vt = _even_proj(x2, attn_norm[0][None, :], ev_w_in[0].astype(_BF16),
                        _even_rope_tables(seq), seq)
    qk3 = qk.reshape(bsz, seq, -1)
    o_a = _moba(qk3, vt).reshape(n, -1)
    lam_rows = jnp.stack([ev_lambda_q1[0], ev_lambda_k1[0], ev_lambda_q2[0], ev_lambda_k2[0]])
    o_b = _diff(qk3, vt, lam_rows, ev_subln[0][None, :], lam_init).reshape(n, -1)
    w_out = ev_w_out[0].astype(_BF16)
    wa, wb = w_out[:o_a.shape[1]], w_out[o_a.shape[1]:]
    wr, br = _router_weights(moe_w_group[0], moe_b_group[0], moe_w_expert[0], moe_b_expert[0])
    x2, t, gates = _out_proj(x2, [(o_a, wa), (o_b, wb)], ffn_norm[0][None, :], wr, br)
    x2 = _moe(x2, t, gates, moe_w_gate[0].astype(_BF16), moe_w_up[0].astype(_BF16),
              moe_w_down[0].astype(_BF16), fg, False)

    w1, wq, wk, wv = _mla_weights(od_w_in[0], od_w_q_up[0], od_w_kv_up[0])
    tabs = _mla_rope_tables(seq, (MLA_NOPE + MLA_ROPE) ** -0.5 * LOG2E)
    q, k, vt = _mla_proj(x2, attn_norm[1][None, :], w1, od_q_norm[0][None, :],
                         od_kv_norm[0][None, :], wq, wk, wv, tabs, seq)
    o_c = _mla(q.reshape(bsz, seq, -1), k.reshape(bsz, seq, -1), vt)
    wr, br = _router_weights(moe_w_group[1], moe_b_group[1], moe_w_expert[1], moe_b_expert[1])
    x2, t, gates = _out_proj(x2, [(o_c.reshape(n, -1), od_w_out[0].astype(_BF16))],
                             ffn_norm[1][None, :], wr, br)
    out = _moe(x2, t, gates, moe_w_gate[1].astype(_BF16), moe_w_up[1].astype(_BF16),
               moe_w_down[1].astype(_BF16), fg, True)
    return out.reshape(bsz, seq, d)
```

```python
import functools
import math

import jax
import jax.numpy as jnp
from jax import lax
from jax.experimental import pallas as pl
from jax.experimental.pallas import tpu as pltpu

D_MODEL = 1024
ROPE_THETA = 10000.0
NORM_EPS = 1e-6

HEAD_DIM = 64
MOBA_HEADS = 8
MOBA_BLOCK = 256
MOBA_TOPK = 3
DIFF_HEADS = 4
EVEN_IN = 3072

MLA_HEADS = 16
MLA_Q_RANK = 256
MLA_KV_RANK = 128
MLA_NOPE = 64
MLA_ROPE = 32
MLA_V = 64

N_GROUPS = 4
EXPERTS_PER_GROUP = 8
N_EXPERTS = 32
EXPERT_FF = 256

LANES = 128
LOG2E = math.log2(math.e)
MASK_BIAS = -1e30
VMEM_LIMIT = 48 * 1024 * 1024

TOKEN_TILE = 512
MOE_TILE = 1024
Q_TILE = 256
KV_TILE = 1024
SUB_TILE = 256

_F32 = jnp.float32
_BF16 = jnp.bfloat16


def _dot(a, b):
    return jnp.dot(a, b, preferred_element_type=_F32)


def _rms(x, g):
    return x * lax.rsqrt(jnp.mean(x * x, axis=-1, keepdims=True) + NORM_EPS) * g


def _params(*sem):
    return pltpu.CompilerParams(dimension_semantics=sem, vmem_limit_bytes=VMEM_LIMIT)


def _rope_tables(seq, dim):
    inv = 1.0 / (ROPE_THETA ** (jnp.arange(0, dim, 2, dtype=_F32) / dim))
    ang = jnp.arange(seq, dtype=_F32)[:, None] * inv[None, :]
    return jnp.cos(ang), jnp.sin(ang)


def _even_rope_tables(seq):
    cos, sin = _rope_tables(seq, HEAD_DIM)
    half = HEAD_DIM // 2
    lane = jnp.arange(LANES)
    first = (lane % HEAD_DIM) < half
    c = cos[:, lane % half]
    s = sin[:, lane % half]
    return c, jnp.where(first, -s, 0.0), jnp.where(first, 0.0, s)


def _mla_rope_tables(seq, scale):
    cos, sin = _rope_tables(seq, MLA_ROPE)
    half = MLA_ROPE // 2
    lane = jnp.arange(LANES)
    r = (lane - MLA_NOPE) % half
    in_rope = (lane >= MLA_NOPE) & (lane < MLA_NOPE + MLA_ROPE)
    first = (lane >= MLA_NOPE) & (lane < MLA_NOPE + half)
    second = (lane >= MLA_NOPE + half) & (lane < MLA_NOPE + MLA_ROPE)
    c, s = cos[:, r], sin[:, r]
    cq = jnp.where(lane < MLA_NOPE, 1.0, jnp.where(in_rope, c, 0.0)) * scale
    ck = jnp.where(in_rope, c, 0.0)
    sa = jnp.where(first, -s, 0.0)
    sb = jnp.where(second, s, 0.0)
    return cq, sa * scale, sb * scale, ck, sa, sb


def _rotate(z, c, sa, sb, half):
    return z * c + pltpu.roll(z, LANES - half, 1) * sa + pltpu.roll(z, half, 1) * sb


def _store_transposed(vt_ref, row0, z):
    vt_ref[row0:row0 + LANES, :] = z.T.astype(_BF16)


def _vt_spec(features, tm):
    per_slab = KV_TILE // tm
    return pl.BlockSpec((None, features, tm), lambda i: (i // per_slab, 0, i % per_slab))


def _even_proj_kernel(x_ref, g_ref, w_ref, c_ref, sa_ref, sb_ref, qk_ref, vt_ref):
    h = _rms(x_ref[...], g_ref[...]).astype(_BF16)
    c, sa, sb = c_ref[...], sa_ref[...], sb_ref[...]
    chunk = 512
    blocks = chunk // LANES
    qk_chunk = {0: 0, 1: 1, 3: 2, 4: 3}
    v_chunk = {2: 0, 5: 1}
    for ci in range(EVEN_IN // chunk):
        z = _dot(h, w_ref[:, ci * chunk:(ci + 1) * chunk])
        for b in range(blocks):
            zz = z[:, b * LANES:(b + 1) * LANES]
            if ci in v_chunk:
                _store_transposed(vt_ref, (v_chunk[ci] * blocks + b) * LANES, zz)
                continue
            zz = _rotate(zz, c, sa, sb, HEAD_DIM // 2)
            if ci in (0, 3):
                zz = zz * (HEAD_DIM ** -0.5 * LOG2E)
            col = (qk_chunk[ci] * blocks + b) * LANES
            qk_ref[:, col:col + LANES] = zz.astype(_BF16)


def _even_proj(x2, g, w, tabs, seq):
    n = x2.shape[0]
    tm = TOKEN_TILE
    spb = seq // tm
    tab_spec = pl.BlockSpec((tm, LANES), lambda i: (i % spb, 0))
    qk_w = 4 * 512
    v_w = 2 * 512
    return pl.pallas_call(
        _even_proj_kernel,
        grid=(n // tm,),
        in_specs=[
            pl.BlockSpec((tm, D_MODEL), lambda i: (i, 0)),
            pl.BlockSpec((1, D_MODEL), lambda i: (0, 0)),
            pl.BlockSpec((D_MODEL, EVEN_IN), lambda i: (0, 0)),
            tab_spec, tab_spec, tab_spec,
        ],
        out_specs=[pl.BlockSpec((tm, qk_w), lambda i: (i, 0)), _vt_spec(v_w, tm)],
        out_shape=[jax.ShapeDtypeStruct((n, qk_w), _BF16),
                   jax.ShapeDtypeStruct((n // KV_TILE, v_w, KV_TILE), _BF16)],
        compiler_params=_params("parallel"),
        name="even_proj",
    )(x2, g, w, *tabs)


class _AttnRefs:
    def __init__(self, s_ref, pm_ref, m_ref, l_ref, acc_ref):
        self.s, self.pm, self.m, self.l, self.acc = s_ref, pm_ref, m_ref, l_ref, acc_ref


def _attn_scratch(dv):
    return [pltpu.VMEM((2, 2, KV_TILE, Q_TILE), _F32),
            pltpu.VMEM((2, 2, 8, Q_TILE), _F32),
            pltpu.VMEM((2, 1, Q_TILE), _F32),
            pltpu.VMEM((2, 1, Q_TILE), _F32),
            pltpu.VMEM((2, dv, Q_TILE), _F32)]


def _kv_rows(c):
    return pl.ds(pl.multiple_of(c * KV_TILE, KV_TILE), KV_TILE)


def _sub_rows(c, r):
    return pl.ds(pl.multiple_of(c * KV_TILE + r * SUB_TILE, SUB_TILE), SUB_TILE)


def _fold8(x, op):
    return op(x.reshape(x.shape[0] // 8, 8, x.shape[1]), axis=0)


def _stage_scores(refs, slot, c, score, mask):
    for t in range(2):
        pm = None
        for r in range(KV_TILE // SUB_TILE):
            s = score(t, c, r)
            if mask is not None:
                s = mask(s, c, r)
            refs.s[slot, t, r * SUB_TILE:(r + 1) * SUB_TILE, :] = s
            part = _fold8(s, jnp.max)
            pm = part if pm is None else jnp.maximum(pm, part)
        refs.pm[slot, t] = pm


def _stage_softmax(refs, slot, c, value):
    for t in range(2):
        m_prev = refs.m[t]
        m_new = jnp.maximum(m_prev, jnp.max(refs.pm[slot, t], axis=0, keepdims=True))
        alpha = jnp.exp2(m_prev - m_new)
        acc = alpha * refs.acc[t]
        lsum = None
        for r in range(KV_TILE // SUB_TILE):
            p = jnp.exp2(refs.s[slot, t, r * SUB_TILE:(r + 1) * SUB_TILE, :] - m_new)
            part = _fold8(p, jnp.sum)
            lsum = part if lsum is None else lsum + part
            acc = acc + _dot(value(t, c, r), p.astype(_BF16))
        refs.m[t] = m_new
        refs.l[t] = alpha * refs.l[t] + jnp.sum(lsum, axis=0, keepdims=True)
        refs.acc[t] = acc


def _stage_both(refs, slot_in, c_in, slot_out, c_out, score, value):
    n_sub = KV_TILE // SUB_TILE
    m_new, alpha, acc, lsum, pm = [], [], [], [None, None], [None, None]
    for t in range(2):
        m_prev = refs.m[t]
        m_new.append(jnp.maximum(m_prev, jnp.max(refs.pm[slot_in, t], axis=0, keepdims=True)))
        alpha.append(jnp.exp2(m_prev - m_new[t]))
        acc.append(alpha[t] * refs.acc[t])
    for r in range(n_sub):
        for t in range(2):
            rows = slice(r * SUB_TILE, (r + 1) * SUB_TILE)
            s = score(t, c_out, r)
            refs.s[slot_out, t, rows, :] = s
            part = _fold8(s, jnp.max)
            pm[t] = part if pm[t] is None else jnp.maximum(pm[t], part)
            p = jnp.exp2(refs.s[slot_in, t, rows, :] - m_new[t])
            part = _fold8(p, jnp.sum)
            lsum[t] = part if lsum[t] is None else lsum[t] + part
            acc[t] = acc[t] + _dot(value(t, c_in, r), p.astype(_BF16))
    for t in range(2):
        refs.pm[slot_out, t] = pm[t]
        refs.m[t] = m_new[t]
        refs.l[t] = alpha[t] * refs.l[t] + jnp.sum(lsum[t], axis=0, keepdims=True)
        refs.acc[t] = acc[t]


def _sweep(refs, qi, score, value):
    last = (qi * Q_TILE) // KV_TILE
    refs.m[...] = jnp.full(refs.m.shape, -jnp.inf, _F32)
    refs.l[...] = jnp.zeros(refs.l.shape, _F32)
    refs.acc[...] = jnp.zeros(refs.acc.shape, _F32)

    def causal(s, c, r):
        key = lax.broadcasted_iota(jnp.int32, s.shape, 0) + (c * KV_TILE + r * SUB_TILE)
        qry = lax.broadcasted_iota(jnp.int32, s.shape, 1) + qi * Q_TILE
        return jnp.where(key <= qry, s, -jnp.inf)

    _stage_scores(refs, 0, last, score, causal)

    def pair(k, carry):
        i = 2 * k
        _stage_both(refs, 0, jnp.where(k == 0, last, i - 1), 1, i, score, value)
        _stage_both(refs, 1, i, 0, i + 1, score, value)
        return carry

    lax.fori_loop(0, last // 2, pair, 0)

    @pl.when(last % 2 == 1)
    def _():
        _stage_both(refs, 0, jnp.where(last == 1, last, last - 2), 1, last - 1, score, value)

    _stage_softmax(refs, last % 2, jnp.maximum(last - 1, 0), value)


def _transposed(q):
    return q.astype(_F32).T


def _moba_kernel(q_ref, k_ref, vt_ref, o_ref, km_ref, ka_ref, *scratch, nb, topk):
    qi = pl.program_id(2)
    tq = Q_TILE
    half = HEAD_DIM // 2

    @pl.when(qi == 0)
    def _():
        kf = k_ref[...].astype(_F32)
        means = jnp.sum(kf.reshape(nb, MOBA_BLOCK, LANES), axis=1) * (1.0 / MOBA_BLOCK)
        lane_b = lax.broadcasted_iota(jnp.int32, (nb, LANES), 1)
        km_ref[...] = jnp.zeros(km_ref.shape, _F32)
        km_ref[0:nb, :] = jnp.where(lane_b < HEAD_DIM, 0.0, means)
        km_ref[HEAD_DIM:HEAD_DIM + nb, :] = jnp.where(lane_b < HEAD_DIM, means, 0.0)

        lane = lax.broadcasted_iota(jnp.int32, (KV_TILE, LANES), 1)
        blk = lax.broadcasted_iota(jnp.int32, (KV_TILE, LANES), 0) // MOBA_BLOCK

        def build(c, carry):
            rows = _kv_rows(c)
            k = k_ref[rows, :].astype(_F32)
            b = blk + c * (KV_TILE // MOBA_BLOCK)
            ka_ref[0, rows, :] = jnp.where(lane < HEAD_DIM, k,
                                           jnp.where(lane == b + HEAD_DIM, 1.0, 0.0)).astype(_BF16)
            ka_ref[1, rows, :] = jnp.where(lane < HEAD_DIM,
                                           jnp.where(lane == b, 1.0, 0.0), k).astype(_BF16)
            return carry

        lax.fori_loop(0, k_ref.shape[0] // KV_TILE, build, 0)

    qt = _transposed(q_ref[...])
    gate = _dot(km_ref[...].astype(_BF16), qt.astype(_BF16))
    row = lax.broadcasted_iota(jnp.int32, (half, tq), 0).astype(_F32)
    own = qi.astype(_F32)

    def block_bias(base):
        g = jnp.where(row < own, gate[base:base + half], -jnp.inf)
        bias = jnp.where(row == own, 0.0, MASK_BIAS)
        for _ in range(topk):
            mx = jnp.max(g, axis=0, keepdims=True)
            hit = (g == mx) & (mx > -jnp.inf)
            idx = jnp.min(jnp.where(hit, row, 4.0 * LANES), axis=0, keepdims=True)
            pick = row == idx
            bias = jnp.where(pick, 0.0, bias)
            g = jnp.where(pick, -jnp.inf, g)
        return bias

    z_half = jnp.zeros((half, tq), _F32)
    qs = (jnp.concatenate([qt[:HEAD_DIM], block_bias(HEAD_DIM), z_half], axis=0).astype(_BF16),
          jnp.concatenate([block_bias(0), z_half, qt[HEAD_DIM:]], axis=0).astype(_BF16))

    def score(t, c, r):
        return _dot(ka_ref[t, _sub_rows(c, r), :], qs[t])

    def value(t, c, r):
        return vt_ref[c, t * HEAD_DIM:(t + 1) * HEAD_DIM, r * SUB_TILE:(r + 1) * SUB_TILE]

    refs = _AttnRefs(*scratch)
    _sweep(refs, qi, score, value)
    ot = jnp.concatenate([refs.acc[0] / refs.l[0], refs.acc[1] / refs.l[1]], axis=0)
    o_ref[...] = ot.T.astype(o_ref.dtype)


def _moba(qk3, vt):
    bsz, seq, _ = qk3.shape
    tq = Q_TILE
    assert MOBA_BLOCK == tq and seq % KV_TILE == 0
    nb = seq // MOBA_BLOCK
    assert nb <= HEAD_DIM // 2
    topk = max(1, min(MOBA_TOPK, nb - 1))
    pairs = MOBA_HEADS // 2
    return pl.pallas_call(
        functools.partial(_moba_kernel, nb=nb, topk=topk),
        grid=(bsz, pairs, seq // tq),
        in_specs=[
            pl.BlockSpec((None, tq, LANES), lambda b, p, i: (b, i, p)),
            pl.BlockSpec((None, seq, LANES), lambda b, p, i: (b, 0, pairs + p)),
            pl.BlockSpec((seq // KV_TILE, LANES, KV_TILE), lambda b, p, i: (b, p, 0)),
        ],
        out_specs=pl.BlockSpec((None, tq, LANES), lambda b, p, i: (b, i, p)),
        out_shape=jax.ShapeDtypeStruct((bsz, seq, pairs * LANES), _BF16),
        scratch_shapes=[pltpu.VMEM((LANES, LANES), _F32), pltpu.VMEM((2, seq, LANES), _BF16)]
                       + _attn_scratch(HEAD_DIM),
        compiler_params=_params("parallel", "parallel", "arbitrary"),
        name="moba_attn",
    )(qk3, qk3, vt)


def _diff_kernel(q_ref, k_ref, vt_ref, lam_ref, g_ref, o_ref, *scratch, lam_init):
    qi = pl.program_id(2)
    tq = Q_TILE
    qt = _transposed(q_ref[...])
    z_head = jnp.zeros((HEAD_DIM, tq), _F32)
    qd = (jnp.concatenate([qt[:HEAD_DIM], z_head], axis=0).astype(_BF16),
          jnp.concatenate([z_head, qt[HEAD_DIM:]], axis=0).astype(_BF16))

    def score(t, c, r):
        return _dot(k_ref[_sub_rows(c, r), :], qd[t])

    def value(t, c, r):
        return vt_ref[c, :, r * SUB_TILE:(r + 1) * SUB_TILE]

    refs = _AttnRefs(*scratch)
    _sweep(refs, qi, score, value)

    lv = lam_ref[...]
    lam = (jnp.exp(jnp.sum(lv[0:1] * lv[1:2], axis=1, keepdims=True))
           - jnp.exp(jnp.sum(lv[2:3] * lv[3:4], axis=1, keepdims=True)) + lam_init)
    ot = refs.acc[0] / refs.l[0] - lam * (refs.acc[1] / refs.l[1])
    o_ref[...] = (_rms(ot.T, g_ref[...]) * (1.0 - lam_init)).astype(o_ref.dtype)


def _diff(qk3, vt, lam_rows, subln, lam_init):
    bsz, seq, _ = qk3.shape
    tq = Q_TILE
    base = 2 * MOBA_HEADS * HEAD_DIM // LANES
    vbase = MOBA_HEADS * HEAD_DIM // LANES
    return pl.pallas_call(
        functools.partial(_diff_kernel, lam_init=lam_init),
        grid=(bsz, DIFF_HEADS, seq // tq),
        in_specs=[
            pl.BlockSpec((None, tq, LANES), lambda b, h, i: (b, i, base + h)),
            pl.BlockSpec((None, seq, LANES), lambda b, h, i: (b, 0, base + DIFF_HEADS + h)),
            pl.BlockSpec((seq // KV_TILE, LANES, KV_TILE), lambda b, h, i: (b, vbase + h, 0)),
            pl.BlockSpec((4, HEAD_DIM), lambda b, h, i: (0, 0)),
            pl.BlockSpec((1, LANES), lambda b, h, i: (0, 0)),
        ],
        out_specs=pl.BlockSpec((None, tq, LANES), lambda b, h, i: (b, i, h)),
        out_shape=jax.ShapeDtypeStruct((bsz, seq, DIFF_HEADS * LANES), _BF16),
        scratch_shapes=_attn_scratch(LANES),
        compiler_params=_params("parallel", "parallel", "arbitrary"),
        name="diff_attn",
    )(qk3, qk3, vt, lam_rows, subln)


def _mla_proj_kernel(x_ref, g_ref, w1_ref, qg_ref, kg_ref, wq_ref, wk_ref, wv_ref,
                     cq_ref, saq_ref, sbq_ref, ck_ref, sak_ref, sbk_ref,
                     q_out, k_out, vt_out):
    h = _rms(x_ref[...], g_ref[...]).astype(_BF16)
    z = _dot(h, w1_ref[...])
    cq = _rms(z[:, :MLA_Q_RANK], qg_ref[...]).astype(_BF16)
    ckv = _rms(z[:, MLA_Q_RANK:MLA_Q_RANK + MLA_KV_RANK], kg_ref[...]).astype(_BF16)
    kr = z[:, MLA_Q_RANK + MLA_KV_RANK:]
    half = MLA_ROPE // 2
    kr = _rotate(kr, ck_ref[...], sak_ref[...], sbk_ref[...], half)
    qf = _dot(cq, wq_ref[...])
    kf = _dot(ckv, wk_ref[...])
    cqt, saq, sbq = cq_ref[...], saq_ref[...], sbq_ref[...]
    for hd in range(MLA_HEADS):
        sl = slice(hd * LANES, (hd + 1) * LANES)
        q_out[:, sl] = _rotate(qf[:, sl], cqt, saq, sbq, half).astype(_BF16)
        k_out[:, sl] = (kf[:, sl] + kr).astype(_BF16)
    vf = _dot(ckv, wv_ref[...])
    for b in range(MLA_HEADS * MLA_V // LANES):
        _store_transposed(vt_out, b * LANES, vf[:, b * LANES:(b + 1) * LANES])


def _mla_proj(x2, g, w1, qg, kg, wq, wk, wv, tabs, seq):
    n = x2.shape[0]
    tm = TOKEN_TILE
    spb = seq // tm
    hw = MLA_HEADS * LANES
    vw = MLA_HEADS * MLA_V
    tab_spec = pl.BlockSpec((tm, LANES), lambda i: (i % spb, 0))

    def full(a):
        return pl.BlockSpec(a.shape, lambda i: (0,) * a.ndim)

    return pl.pallas_call(
        _mla_proj_kernel,
        grid=(n // tm,),
        in_specs=[pl.BlockSpec((tm, D_MODEL), lambda i: (i, 0)),
                  full(g), full(w1), full(qg), full(kg), full(wq), full(wk), full(wv)]
                 + [tab_spec] * 6,
        out_specs=[pl.BlockSpec((tm, hw), lambda i: (i, 0)),
                   pl.BlockSpec((tm, hw), lambda i: (i, 0)),
                   _vt_spec(vw, tm)],
        out_shape=[jax.ShapeDtypeStruct((n, hw), _BF16),
                   jax.ShapeDtypeStruct((n, hw), _BF16),
                   jax.ShapeDtypeStruct((n // KV_TILE, vw, KV_TILE), _BF16)],
        compiler_params=_params("parallel"),
        name="mla_proj",
    )(x2, g, w1, qg, kg, wq, wk, wv, *tabs)


def _mla_kernel(q_ref, k_ref, vt_ref, o_ref, *scratch):
    qi = pl.program_id(2)
    q = q_ref[...]
    qd = tuple(_transposed(q[:, t * LANES:(t + 1) * LANES]).astype(_BF16) for t in range(2))

    def score(t, c, r):
        return _dot(k_ref[_sub_rows(c, r), t * LANES:(t + 1) * LANES], qd[t])

    def value(t, c, r):
        return vt_ref[c, t * MLA_V:(t + 1) * MLA_V, r * SUB_TILE:(r + 1) * SUB_TILE]

    refs = _AttnRefs(*scratch)
    _sweep(refs, qi, score, value)
    ot = jnp.concatenate([refs.acc[0] / refs.l[0], refs.acc[1] / refs.l[1]], axis=0)
    o_ref[...] = ot.T.astype(o_ref.dtype)


def _mla(q3, k3, vt):
    bsz, seq, _ = q3.shape
    tq = Q_TILE
    pairs = MLA_HEADS // 2
    return pl.pallas_call(
        _mla_kernel,
        grid=(bsz, pairs, seq // tq),
        in_specs=[
            pl.BlockSpec((None, tq, 2 * LANES), lambda b, p, i: (b, i, p)),
            pl.BlockSpec((None, seq, 2 * LANES), lambda b, p, i: (b, 0, p)),
            pl.BlockSpec((seq // KV_TILE, LANES, KV_TILE), lambda b, p, i: (b, p, 0)),
        ],
        out_specs=pl.BlockSpec((None, tq, LANES), lambda b, p, i: (b, i, p)),
        out_shape=jax.ShapeDtypeStruct((bsz, seq, pairs * LANES), _BF16),
        scratch_shapes=_attn_scratch(MLA_V),
        compiler_params=_params("parallel", "parallel", "arbitrary"),
        name="mla_attn",
    )(q3, k3, vt)


def _route(logits):
    lane = lax.broadcasted_iota(jnp.int32, logits.shape, 1).astype(_F32)
    big = 4.0 * LANES
    is_group = (lane >= N_EXPERTS) & (lane < N_EXPERTS + N_GROUPS)
    gl = jnp.where(is_group, logits, -jnp.inf)
    gmax = jnp.max(gl, axis=1, keepdims=True)
    grp = jnp.min(jnp.where(gl == gmax, lane, big), axis=1, keepdims=True) - N_EXPERTS
    p_group = 1.0 / jnp.sum(jnp.exp(gl - gmax), axis=1, keepdims=True)
    in_group = (lane >= grp * EXPERTS_PER_GROUP) & (lane < (grp + 1) * EXPERTS_PER_GROUP)
    el = jnp.where(in_group, logits, -jnp.inf)
    e1 = jnp.max(el, axis=1, keepdims=True)
    i1 = jnp.min(jnp.where(el == e1, lane, big), axis=1, keepdims=True)
    el2 = jnp.where(lane == i1, -jnp.inf, el)
    e2 = jnp.max(el2, axis=1, keepdims=True)
    i2 = jnp.min(jnp.where(el2 == e2, lane, big), axis=1, keepdims=True)
    r = jnp.exp(e2 - e1)
    w1 = p_group / (1.0 + r)
    w2 = w1 * r
    return jnp.where(lane == i1, w1, 0.0) + jnp.where(lane == i2, w2, 0.0)


def _out_proj_kernel(*refs, n_mix):
    x_ref = refs[0]
    mix = refs[1:1 + 2 * n_mix]
    g_ref, wr_ref, br_ref = refs[1 + 2 * n_mix:4 + 2 * n_mix]
    x_out, t_out, gate_out = refs[4 + 2 * n_mix:]
    x = x_ref[...]
    for i in range(n_mix):
        x = x + _dot(mix[2 * i][...], mix[2 * i + 1][...])
    x_out[...] = x
    t = _rms(x, g_ref[...]).astype(_BF16)
    t_out[...] = t
    gate_out[...] = _route(_dot(t, wr_ref[...]) + br_ref[...])


def _out_proj(x2, mixes, g, wr, br):
    n = x2.shape[0]
    tm = TOKEN_TILE
    in_specs = [pl.BlockSpec((tm, D_MODEL), lambda i: (i, 0))]
    args = [x2]
    for o, w in mixes:
        in_specs.append(pl.BlockSpec((tm, o.shape[1]), lambda i: (i, 0)))
        in_specs.append(pl.BlockSpec(w.shape, lambda i: (0, 0)))
        args += [o, w]
    for a in (g, wr, br):
        in_specs.append(pl.BlockSpec(a.shape, lambda i: (0, 0)))
        args.append(a)
    return pl.pallas_call(
        functools.partial(_out_proj_kernel, n_mix=len(mixes)),
        grid=(n // tm,),
        in_specs=in_specs,
        out_specs=[pl.BlockSpec((tm, D_MODEL), lambda i: (i, 0)),
                   pl.BlockSpec((tm, D_MODEL), lambda i: (i, 0)),
                   pl.BlockSpec((tm, LANES), lambda i: (i, 0))],
        out_shape=[jax.ShapeDtypeStruct((n, D_MODEL), _F32),
                   jax.ShapeDtypeStruct((n, D_MODEL), _BF16),
                   jax.ShapeDtypeStruct((n, LANES), _F32)],
        compiler_params=_params("parallel"),
        name="out_proj_route",
    )(*args)


def _moe_kernel(x_ref, t_ref, gate_ref, wg_ref, wu_ref, wd_ref, fg_ref, o_ref, acc_ref, *, final):
    e = pl.program_id(1)

    @pl.when(e == 0)
    def _():
        acc_ref[...] = x_ref[...]

    t = t_ref[...]
    gates = gate_ref[...]
    lane = lax.broadcasted_iota(jnp.int32, gates.shape, 1)
    ge = jnp.sum(jnp.where(lane == e, gates, 0.0), axis=1, keepdims=True)
    hg = _dot(t, wg_ref[...])
    hu = _dot(t, wu_ref[...])
    act = hg * (1.0 / (1.0 + jnp.exp(-hg))) * hu * ge
    acc_ref[...] += _dot(act.astype(_BF16), wd_ref[...])

    @pl.when(e == pl.num_programs(1) - 1)
    def _():
        y = acc_ref[...]
        if final:
            y = _rms(y, fg_ref[...])
        o_ref[...] = y


def _moe(x2, t, gates, wg, wu, wd, fg, final):
    n = x2.shape[0]
    tm = min(MOE_TILE, n)
    return pl.pallas_call(
        functools.partial(_moe_kernel, final=final),
        grid=(n // tm, N_EXPERTS),
        in_specs=[
            pl.BlockSpec((tm, D_MODEL), lambda i, e: (i, 0)),
            pl.BlockSpec((tm, D_MODEL), lambda i, e: (i, 0)),
            pl.BlockSpec((tm, LANES), lambda i, e: (i, 0)),
            pl.BlockSpec((None, D_MODEL, EXPERT_FF), lambda i, e: (e, 0, 0)),
            pl.BlockSpec((None, D_MODEL, EXPERT_FF), lambda i, e: (e, 0, 0)),
            pl.BlockSpec((None, EXPERT_FF, D_MODEL), lambda i, e: (e, 0, 0)),
            pl.BlockSpec((1, D_MODEL), lambda i, e: (0, 0)),
        ],
        out_specs=pl.BlockSpec((tm, D_MODEL), lambda i, e: (i, 0)),
        out_shape=jax.ShapeDtypeStruct((n, D_MODEL), _F32),
        scratch_shapes=[pltpu.VMEM((tm, D_MODEL), _F32)],
        compiler_params=_params("parallel", "arbitrary"),
        name="moe_experts",
    )(x2, t, gates, wg, wu, wd, fg)


def _router_weights(w_group, b_group, w_expert, b_expert):
    pad = LANES - N_EXPERTS - N_GROUPS
    wr = jnp.concatenate([w_expert, w_group, jnp.zeros((D_MODEL, pad), _F32)], axis=1)
    br = jnp.concatenate([b_expert, b_group, jnp.zeros((pad,), _F32)])[None, :]
    return wr.astype(_BF16), br


def _mla_weights(w_in, w_q_up, w_kv_up):
    lat = MLA_Q_RANK + MLA_KV_RANK
    w1 = jnp.concatenate([w_in[:, :lat], jnp.zeros((D_MODEL, MLA_NOPE), _F32), w_in[:, lat:],
                          jnp.zeros((D_MODEL, LANES - MLA_NOPE - MLA_ROPE), _F32)], axis=1)
    qd = MLA_NOPE + MLA_ROPE
    wq = jnp.pad(w_q_up.reshape(MLA_Q_RANK, MLA_HEADS, qd), ((0, 0), (0, 0), (0, LANES - qd)))
    kv = w_kv_up.reshape(MLA_KV_RANK, MLA_HEADS, MLA_NOPE + MLA_V)
    wk = jnp.pad(kv[:, :, :MLA_NOPE], ((0, 0), (0, 0), (0, LANES - MLA_NOPE)))
    wv = kv[:, :, MLA_NOPE:]
    return (w1.astype(_BF16), wq.reshape(MLA_Q_RANK, -1).astype(_BF16),
            wk.reshape(MLA_KV_RANK, -1).astype(_BF16), wv.reshape(MLA_KV_RANK, -1).astype(_BF16))


def kernel(x, attn_norm, ev_w_in, ev_lambda_q1, ev_lambda_k1, ev_lambda_q2, ev_lambda_k2, ev_subln, ev_w_out, od_w_in, od_q_norm, od_kv_norm, od_w_q_up, od_w_kv_up, od_w_out, ffn_norm, moe_w_group, moe_b_group, moe_w_expert, moe_b_expert, moe_w_gate, moe_w_up, moe_w_down, final_norm):
    bsz, seq, d = x.shape
    n = bsz * seq
    x2 = x.reshape(n, d)
    fg = final_norm[None, :]

    lam_init = 0.8 - 0.6 * math.exp(-0.3 * 0)
    qk, vt = _even_proj(x2, attn_norm[0][None, :], ev_w_in[0].astype(_BF16),
                        _even_rope_tables(seq), seq)
    qk3 = qk.reshape(bsz, seq, -1)
    o_a = _moba(qk3, vt).reshape(n, -1)
    lam_rows = jnp.stack([ev_lambda_q1[0], ev_lambda_k1[0], ev_lambda_q2[0], ev_lambda_k2[0]])
    o_b = _diff(qk3, vt, lam_rows, ev_subln[0][None, :], lam_init).reshape(n, -1)
    w_out = ev_w_out[0].astype(_BF16)
    wa, wb = w_out[:o_a.shape[1]], w_out[o_a.shape[1]:]
    wr, br = _router_weights(moe_w_group[0], moe_b_group[0], moe_w_expert[0], moe_b_expert[0])
    x2, t, gates = _out_proj(x2, [(o_a, wa), (o_b, wb)], ffn_norm[0][None, :], wr, br)
    x2 = _moe(x2, t, gates, moe_w_gate[0].astype(_BF16), moe_w_up[0].astype(_BF16),
              moe_w_down[0].astype(_BF16), fg, False)

    w1, wq, wk, wv = _mla_weights(od_w_in[0], od_w_q_up[0], od_w_kv_up[0])
    tabs = _mla_rope_tables(seq, (MLA_NOPE + MLA_ROPE) ** -0.5 * LOG2E)
    q, k, vt = _mla_proj(x2, attn_norm[1][None, :], w1, od_q_norm[0][None, :],
                         od_kv_norm[0][None, :], wq, wk, wv, tabs, seq)
    o_c = _mla(q.reshape(bsz, seq, -1), k.reshape(bsz, seq, -1), vt)
    wr, br = _router_weights(moe_w_group[1], moe_b_group[1], moe_w_expert[1], moe_b_expert[1])
    x2, t, gates = _out_proj(x2, [(o_c.reshape(n, -1), od_w_out[0].astype(_BF16))],
                             ffn_norm[1][None, :], wr, br)
    out = _moe(x2, t, gates, moe_w_gate[1].astype(_BF16), moe_w_up[1].astype(_BF16),
               moe_w_down[1].astype(_BF16), fg, True)
    return out.reshape(bsz, seq, d)
```

```python
import functools
import math

import jax
import jax.numpy as jnp
from jax import lax
from jax.experimental import pallas as pl
from jax.experimental.pallas import tpu as pltpu

D_MODEL = 1024
ROPE_THETA = 10000.0
NORM_EPS = 1e-6

HEAD_DIM = 64
MOBA_HEADS = 8
MOBA_BLOCK = 256
MOBA_TOPK = 3
DIFF_HEADS = 4
EVEN_IN = 3072

MLA_HEADS = 16
MLA_Q_RANK = 256
MLA_KV_RANK = 128
MLA_NOPE = 64
MLA_ROPE = 32
MLA_V = 64

N_GROUPS = 4
EXPERTS_PER_GROUP = 8
N_EXPERTS = 32
EXPERT_FF = 256
PAIRS_PER_GROUP = EXPERTS_PER_GROUP * (EXPERTS_PER_GROUP - 1) // 2
N_CLASSES = N_GROUPS * PAIRS_PER_GROUP

LANES = 128
LOG2E = math.log2(math.e)
MASK_BIAS = -1e30
VMEM_LIMIT = 48 * 1024 * 1024

TOKEN_TILE = 512
EXPERT_TILE = 256
ROW_W = D_MODEL + 128
ROW_COPY_UNROLL = 8
Q_TILE = 256
KV_TILE = 1024
SUB_TILE = 256

_F32 = jnp.float32
_BF16 = jnp.bfloat16


def _dot(a, b):
    return jnp.dot(a, b, preferred_element_type=_F32)


def _rms(x, g):
    return x * lax.rsqrt(jnp.mean(x * x, axis=-1, keepdims=True) + NORM_EPS) * g


def _params(*sem):
    return pltpu.CompilerParams(dimension_semantics=sem, vmem_limit_bytes=VMEM_LIMIT)


def _rope_tables(seq, dim):
    inv = 1.0 / (ROPE_THETA ** (jnp.arange(0, dim, 2, dtype=_F32) / dim))
    ang = jnp.arange(seq, dtype=_F32)[:, None] * inv[None, :]
    return jnp.cos(ang), jnp.sin(ang)


def _even_rope_tables(seq):
    cos, sin = _rope_tables(seq, HEAD_DIM)
    half = HEAD_DIM // 2
    lane = jnp.arange(LANES)
    first = (lane % HEAD_DIM) < half
    c = cos[:, lane % half]
    s = sin[:, lane % half]
    return c, jnp.where(first, -s, 0.0), jnp.where(first, 0.0, s)


def _mla_rope_tables(seq, scale):
    cos, sin = _rope_tables(seq, MLA_ROPE)
    half = MLA_ROPE // 2
    lane = jnp.arange(LANES)
    r = (lane - MLA_NOPE) % half
    in_rope = (lane >= MLA_NOPE) & (lane < MLA_NOPE + MLA_ROPE)
    first = (lane >= MLA_NOPE) & (lane < MLA_NOPE + half)
    second = (lane >= MLA_NOPE + half) & (lane < MLA_NOPE + MLA_ROPE)
    c, s = cos[:, r], sin[:, r]
    cq = jnp.where(lane < MLA_NOPE, 1.0, jnp.where(in_rope, c, 0.0)) * scale
    ck = jnp.where(in_rope, c, 0.0)
    sa = jnp.where(first, -s, 0.0)
    sb = jnp.where(second, s, 0.0)
    return cq, sa * scale, sb * scale, ck, sa, sb


def _rotate(z, c, sa, sb, half):
    return z * c + pltpu.roll(z, LANES - half, 1) * sa + pltpu.roll(z, half, 1) * sb


def _store_transposed(vt_ref, row0, z):
    vt_ref[row0:row0 + LANES, :] = z.T.astype(_BF16)


def _vt_spec(features, tm):
    per_slab = KV_TILE // tm
    return pl.BlockSpec((None, features, tm), lambda i: (i // per_slab, 0, i % per_slab))


def _even_proj_kernel(x_ref, g_ref, w_ref, c_ref, sa_ref, sb_ref, qk_ref, vt_ref):
    h = _rms(x_ref[...], g_ref[...]).astype(_BF16)
    c, sa, sb = c_ref[...], sa_ref[...], sb_ref[...]
    chunk = 512
    blocks = chunk // LANES
    qk_chunk = {0: 0, 1: 1, 3: 2, 4: 3}
    v_chunk = {2: 0, 5: 1}
    for ci in range(EVEN_IN // chunk):
        z = _dot(h, w_ref[:, ci * chunk:(ci + 1) * chunk])
        for b in range(blocks):
            zz = z[:, b * LANES:(b + 1) * LANES]
            if ci in v_chunk:
                _store_transposed(vt_ref, (v_chunk[ci] * blocks + b) * LANES, zz)
                continue
            zz = _rotate(zz, c, sa, sb, HEAD_DIM // 2)
            if ci in (0, 3):
                zz = zz * (HEAD_DIM ** -0.5 * LOG2E)
            col = (qk_chunk[ci] * blocks + b) * LANES
            qk_ref[:, col:col + LANES] = zz.astype(_BF16)


def _even_proj(x2, g, w, tabs, seq):
    n = x2.shape[0]
    tm = TOKEN_TILE
    spb = seq // tm
    tab_spec = pl.BlockSpec((tm, LANES), lambda i: (i % spb, 0))
    qk_w = 4 * 512
    v_w = 2 * 512
    return pl.pallas_call(
        _even_proj_kernel,
        grid=(n // tm,),
        in_specs=[
            pl.BlockSpec((tm, D_MODEL), lambda i: (i, 0)),
            pl.BlockSpec((1, D_MODEL), lambda i: (0, 0)),
            pl.BlockSpec((D_MODEL, EVEN_IN), lambda i: (0, 0)),
            tab_spec, tab_spec, tab_spec,
        ],
        out_specs=[pl.BlockSpec((tm, qk_w), lambda i: (i, 0)), _vt_spec(v_w, tm)],
        out_shape=[jax.ShapeDtypeStruct((n, qk_w), _BF16),
                   jax.ShapeDtypeStruct((n // KV_TILE, v_w, KV_TILE), _BF16)],
        compiler_params=_params("parallel"),
        name="even_proj",
    )(x2, g, w, *tabs)


class _AttnRefs:
    def __init__(self, s_ref, pm_ref, m_ref, l_ref, acc_ref):
        self.s, self.pm, self.m, self.l, self.acc = s_ref, pm_ref, m_ref, l_ref, acc_ref


def _attn_scratch(dv):
    return [pltpu.VMEM((2, 2, KV_TILE, Q_TILE), _F32),
            pltpu.VMEM((2, 2, 8, Q_TILE), _F32),
            pltpu.VMEM((2, 1, Q_TILE), _F32),
            pltpu.VMEM((2, 1, Q_TILE), _F32),
            pltpu.VMEM((2, dv, Q_TILE), _F32)]


def _kv_rows(c):
    return pl.ds(pl.multiple_of(c * KV_TILE, KV_TILE), KV_TILE)


def _sub_rows(c, r):
    return pl.ds(pl.multiple_of(c * KV_TILE + r * SUB_TILE, SUB_TILE), SUB_TILE)


def _fold8(x, op):
    return op(x.reshape(x.shape[0] // 8, 8, x.shape[1]), axis=0)


def _stage_scores(refs, slot, c, score, mask):
    for t in range(2):
        pm = None
        for r in range(KV_TILE // SUB_TILE):
            s = score(t, c, r)
            if mask is not None:
                s = mask(s, c, r)
            refs.s[slot, t, r * SUB_TILE:(r + 1) * SUB_TILE, :] = s
            part = _fold8(s, jnp.max)
            pm = part if pm is None else jnp.maximum(pm, part)
        refs.pm[slot, t] = pm


def _stage_softmax(refs, slot, c, value):
    for t in range(2):
        m_prev = refs.m[t]
        m_new = jnp.maximum(m_prev, jnp.max(refs.pm[slot, t], axis=0, keepdims=True))
        alpha = jnp.exp2(m_prev - m_new)
        acc = alpha * refs.acc[t]
        lsum = None
        for r in range(KV_TILE // SUB_TILE):
            p = jnp.exp2(refs.s[slot, t, r * SUB_TILE:(r + 1) * SUB_TILE, :] - m_new)
            part = _fold8(p, jnp.sum)
            lsum = part if lsum is None else lsum + part
            acc = acc + _dot(value(t, c, r), p.astype(_BF16))
        refs.m[t] = m_new
        refs.l[t] = alpha * refs.l[t] + jnp.sum(lsum, axis=0, keepdims=True)
        refs.acc[t] = acc


def _stage_both(refs, slot_in, c_in, slot_out, c_out, score, value):
    n_sub = KV_TILE // SUB_TILE
    m_new, alpha, acc, lsum, pm = [], [], [], [None, None], [None, None]
    for t in range(2):
        m_prev = refs.m[t]
        m_new.append(jnp.maximum(m_prev, jnp.max(refs.pm[slot_in, t], axis=0, keepdims=True)))
        alpha.append(jnp.exp2(m_prev - m_new[t]))
        acc.append(alpha[t] * refs.acc[t])
    for r in range(n_sub):
        for t in range(2):
            rows = slice(r * SUB_TILE, (r + 1) * SUB_TILE)
            s = score(t, c_out, r)
            refs.s[slot_out, t, rows, :] = s
            part = _fold8(s, jnp.max)
            pm[t] = part if pm[t] is None else jnp.maximum(pm[t], part)
            p = jnp.exp2(refs.s[slot_in, t, rows, :] - m_new[t])
            part = _fold8(p, jnp.sum)
            lsum[t] = part if lsum[t] is None else lsum[t] + part
            acc[t] = acc[t] + _dot(value(t, c_in, r), p.astype(_BF16))
    for t in range(2):
        refs.pm[slot_out, t] = pm[t]
        refs.m[t] = m_new[t]
        refs.l[t] = alpha[t] * refs.l[t] + jnp.sum(lsum[t], axis=0, keepdims=True)
        refs.acc[t] = acc[t]


def _sweep(refs, qi, score, value):
    last = (qi * Q_TILE) // KV_TILE
    refs.m[...] = jnp.full(refs.m.shape, -jnp.inf, _F32)
    refs.l[...] = jnp.zeros(refs.l.shape, _F32)
    refs.acc[...] = jnp.zeros(refs.acc.shape, _F32)

    def causal(s, c, r):
        key = lax.broadcasted_iota(jnp.int32, s.shape, 0) + (c * KV_TILE + r * SUB_TILE)
        qry = lax.broadcasted_iota(jnp.int32, s.shape, 1) + qi * Q_TILE
        return jnp.where(key <= qry, s, -jnp.inf)

    _stage_scores(refs, 0, last, score, causal)

    def pair(k, carry):
        i = 2 * k
        _stage_both(refs, 0, jnp.where(k == 0, last, i - 1), 1, i, score, value)
        _stage_both(refs, 1, i, 0, i + 1, score, value)
        return carry

    lax.fori_loop(0, last // 2, pair, 0)

    @pl.when(last % 2 == 1)
    def _():
        _stage_both(refs, 0, jnp.where(last == 1, last, last - 2), 1, last - 1, score, value)

    _stage_softmax(refs, last % 2, jnp.maximum(last - 1, 0), value)


def _transposed(q):
    return q.astype(_F32).T


def _moba_kernel(q_ref, k_ref, vt_ref, o_ref, km_ref, ka_ref, *scratch, nb, topk):
    qi = pl.program_id(2)
    tq = Q_TILE
    half = HEAD_DIM // 2

    @pl.when(qi == 0)
    def _():
        kf = k_ref[...].astype(_F32)
        means = jnp.sum(kf.reshape(nb, MOBA_BLOCK, LANES), axis=1) * (1.0 / MOBA_BLOCK)
        lane_b = lax.broadcasted_iota(jnp.int32, (nb, LANES), 1)
        km_ref[...] = jnp.zeros(km_ref.shape, _F32)
        km_ref[0:nb, :] = jnp.where(lane_b < HEAD_DIM, 0.0, means)
        km_ref[HEAD_DIM:HEAD_DIM + nb, :] = jnp.where(lane_b < HEAD_DIM, means, 0.0)

        lane = lax.broadcasted_iota(jnp.int32, (KV_TILE, LANES), 1)
        blk = lax.broadcasted_iota(jnp.int32, (KV_TILE, LANES), 0) // MOBA_BLOCK

        def build(c, carry):
            rows = _kv_rows(c)
            k = k_ref[rows, :].astype(_F32)
            b = blk + c * (KV_TILE // MOBA_BLOCK)
            ka_ref[0, rows, :] = jnp.where(lane < HEAD_DIM, k,
                                           jnp.where(lane == b + HEAD_DIM, 1.0, 0.0)).astype(_BF16)
            ka_ref[1, rows, :] = jnp.where(lane < HEAD_DIM,
                                           jnp.where(lane == b, 1.0, 0.0), k).astype(_BF16)
            return carry

        lax.fori_loop(0, k_ref.shape[0] // KV_TILE, build, 0)

    qt = _transposed(q_ref[...])
    gate = _dot(km_ref[...].astype(_BF16), qt.astype(_BF16))
    row = lax.broadcasted_iota(jnp.int32, (half, tq), 0).astype(_F32)
    own = qi.astype(_F32)

    def block_bias(base):
        g = jnp.where(row < own, gate[base:base + half], -jnp.inf)
        bias = jnp.where(row == own, 0.0, MASK_BIAS)
        for _ in range(topk):
            mx = jnp.max(g, axis=0, keepdims=True)
            hit = (g == mx) & (mx > -jnp.inf)
            idx = jnp.min(jnp.where(hit, row, 4.0 * LANES), axis=0, keepdims=True)
            pick = row == idx
            bias = jnp.where(pick, 0.0, bias)
            g = jnp.where(pick, -jnp.inf, g)
        return bias

    z_half = jnp.zeros((half, tq), _F32)
    qs = (jnp.concatenate([qt[:HEAD_DIM], block_bias(HEAD_DIM), z_half], axis=0).astype(_BF16),
          jnp.concatenate([block_bias(0), z_half, qt[HEAD_DIM:]], axis=0).astype(_BF16))

    def score(t, c, r):
        return _dot(ka_ref[t, _sub_rows(c, r), :], qs[t])

    def value(t, c, r):
        return vt_ref[c, t * HEAD_DIM:(t + 1) * HEAD_DIM, r * SUB_TILE:(r + 1) * SUB_TILE]

    refs = _AttnRefs(*scratch)
    _sweep(refs, qi, score, value)
    ot = jnp.concatenate([refs.acc[0] / refs.l[0], refs.acc[1] / refs.l[1]], axis=0)
    o_ref[...] = ot.T.astype(o_ref.dtype)


def _moba(qk3, vt):
    bsz, seq, _ = qk3.shape
    tq = Q_TILE
    assert MOBA_BLOCK == tq and seq % KV_TILE == 0
    nb = seq // MOBA_BLOCK
    assert nb <= HEAD_DIM // 2
    topk = max(1, min(MOBA_TOPK, nb - 1))
    pairs = MOBA_HEADS // 2
    return pl.pallas_call(
        functools.partial(_moba_kernel, nb=nb, topk=topk),
        grid=(bsz, pairs, seq // tq),
        in_specs=[
            pl.BlockSpec((None, tq, LANES), lambda b, p, i: (b, i, p)),
            pl.BlockSpec((None, seq, LANES), lambda b, p, i: (b, 0, pairs + p)),
            pl.BlockSpec((seq // KV_TILE, LANES, KV_TILE), lambda b, p, i: (b, p, 0)),
        ],
        out_specs=pl.BlockSpec((None, tq, LANES), lambda b, p, i: (b, i, p)),
        out_shape=jax.ShapeDtypeStruct((bsz, seq, pairs * LANES), _BF16),
        scratch_shapes=[pltpu.VMEM((LANES, LANES), _F32), pltpu.VMEM((2, seq, LANES), _BF16)]
                       + _attn_scratch(HEAD_DIM),
        compiler_params=_params("parallel", "parallel", "arbitrary"),
        name="moba_attn",
    )(qk3, qk3, vt)


def _diff_kernel(q_ref, k_ref, vt_ref, lam_ref, g_ref, o_ref, *scratch, lam_init):
    qi = pl.program_id(2)
    tq = Q_TILE
    qt = _transposed(q_ref[...])
    z_head = jnp.zeros((HEAD_DIM, tq), _F32)
    qd = (jnp.concatenate([qt[:HEAD_DIM], z_head], axis=0).astype(_BF16),
          jnp.concatenate([z_head, qt[HEAD_DIM:]], axis=0).astype(_BF16))

    def score(t, c, r):
        return _dot(k_ref[_sub_rows(c, r), :], qd[t])

    def value(t, c, r):
        return vt_ref[c, :, r * SUB_TILE:(r + 1) * SUB_TILE]

    refs = _AttnRefs(*scratch)
    _sweep(refs, qi, score, value)

    lv = lam_ref[...]
    lam = (jnp.exp(jnp.sum(lv[0:1] * lv[1:2], axis=1, keepdims=True))
           - jnp.exp(jnp.sum(lv[2:3] * lv[3:4], axis=1, keepdims=True)) + lam_init)
    ot = refs.acc[0] / refs.l[0] - lam * (refs.acc[1] / refs.l[1])
    o_ref[...] = (_rms(ot.T, g_ref[...]) * (1.0 - lam_init)).astype(o_ref.dtype)


def _diff(qk3, vt, lam_rows, subln, lam_init):
    bsz, seq, _ = qk3.shape
    tq = Q_TILE
    base = 2 * MOBA_HEADS * HEAD_DIM // LANES
    vbase = MOBA_HEADS * HEAD_DIM // LANES
    return pl.pallas_call(
        functools.partial(_diff_kernel, lam_init=lam_init),
        grid=(bsz, DIFF_HEADS, seq // tq),
        in_specs=[
            pl.BlockSpec((None, tq, LANES), lambda b, h, i: (b, i, base + h)),
            pl.BlockSpec((None, seq, LANES), lambda b, h, i: (b, 0, base + DIFF_HEADS + h)),
            pl.BlockSpec((seq // KV_TILE, LANES, KV_TILE), lambda b, h, i: (b, vbase + h, 0)),
            pl.BlockSpec((4, HEAD_DIM), lambda b, h, i: (0, 0)),
            pl.BlockSpec((1, LANES), lambda b, h, i: (0, 0)),
        ],
        out_specs=pl.BlockSpec((None, tq, LANES), lambda b, h, i: (b, i, h)),
        out_shape=jax.ShapeDtypeStruct((bsz, seq, DIFF_HEADS * LANES), _BF16),
        scratch_shapes=_attn_scratch(LANES),
        compiler_params=_params("parallel", "parallel", "arbitrary"),
        name="diff_attn",
    )(qk3, qk3, vt, lam_rows, subln)


def _mla_proj_kernel(cls_ref, rank_ref, off_ref, x_ref, ys_ref,
                     g_ref, w1_ref, qg_ref, kg_ref, wq_ref, wk_ref, wv_ref,
                     cq_ref, saq_ref, sbq_ref, ck_ref, sak_ref, sbk_ref,
                     x_out, q_out, k_out, vt_out, buf_ref, sem):
    _gather_rows(cls_ref, rank_ref, off_ref, ys_ref, buf_ref, sem)
    x = x_ref[...] + buf_ref[...]
    x_out[...] = x
    h = _rms(x, g_ref[...]).astype(_BF16)
    z = _dot(h, w1_ref[...])
    cq = _rms(z[:, :MLA_Q_RANK], qg_ref[...]).astype(_BF16)
    ckv = _rms(z[:, MLA_Q_RANK:MLA_Q_RANK + MLA_KV_RANK], kg_ref[...]).astype(_BF16)
    kr = z[:, MLA_Q_RANK + MLA_KV_RANK:]
    half = MLA_ROPE // 2
    kr = _rotate(kr, ck_ref[...], sak_ref[...], sbk_ref[...], half)
    qf = _dot(cq, wq_ref[...])
    kf = _dot(ckv, wk_ref[...])
    cqt, saq, sbq = cq_ref[...], saq_ref[...], sbq_ref[...]
    for hd in range(MLA_HEADS):
        sl = slice(hd * LANES, (hd + 1) * LANES)
        q_out[:, sl] = _rotate(qf[:, sl], cqt, saq, sbq, half).astype(_BF16)
        k_out[:, sl] = (kf[:, sl] + kr).astype(_BF16)
    vf = _dot(ckv, wv_ref[...])
    for b in range(MLA_HEADS * MLA_V // LANES):
        _store_transposed(vt_out, b * LANES, vf[:, b * LANES:(b + 1) * LANES])


def _mla_proj(cls, rank, offsets, x2, ys, g, w1, qg, kg, wq, wk, wv, tabs, seq):
    n = x2.shape[0]
    tm = TOKEN_TILE
    spb = seq // tm
    hw = MLA_HEADS * LANES
    vw = MLA_HEADS * MLA_V
    tab_spec = pl.BlockSpec((tm, LANES), lambda i: (i % spb, 0))

    def full(a):
        return pl.BlockSpec(a.shape, lambda i: (0,) * a.ndim)

    return pl.pallas_call(
        _mla_proj_kernel,
        grid=(n // tm,),
        in_specs=_index_specs(tm)
                 + [pl.BlockSpec((tm, D_MODEL), lambda i: (i, 0)),
                    pl.BlockSpec(memory_space=pl.ANY),
                    full(g), full(w1), full(qg), full(kg), full(wq), full(wk), full(wv)]
                 + [tab_spec] * 6,
        out_specs=[pl.BlockSpec((tm, D_MODEL), lambda i: (i, 0)),
                   pl.BlockSpec((tm, hw), lambda i: (i, 0)),
                   pl.BlockSpec((tm, hw), lambda i: (i, 0)),
                   _vt_spec(vw, tm)],
        out_shape=[jax.ShapeDtypeStruct((n, D_MODEL), _F32),
                   jax.ShapeDtypeStruct((n, hw), _BF16),
                   jax.ShapeDtypeStruct((n, hw), _BF16),
                   jax.ShapeDtypeStruct((n // KV_TILE, vw, KV_TILE), _BF16)],
        scratch_shapes=[pltpu.VMEM((tm, D_MODEL), _F32), pltpu.SemaphoreType.DMA(())],
        compiler_params=_params("arbitrary"),
        name="mla_proj",
    )(cls, rank, offsets, x2, ys, g, w1, qg, kg, wq, wk, wv, *tabs)


def _mla_kernel(q_ref, k_ref, vt_ref, o_ref, *scratch):
    qi = pl.program_id(2)
    q = q_ref[...]
    qd = tuple(_transposed(q[:, t * LANES:(t + 1) * LANES]).astype(_BF16) for t in range(2))

    def score(t, c, r):
        return _dot(k_ref[_sub_rows(c, r), t * LANES:(t + 1) * LANES], qd[t])

    def value(t, c, r):
        return vt_ref[c, t * MLA_V:(t + 1) * MLA_V, r * SUB_TILE:(r + 1) * SUB_TILE]

    refs = _AttnRefs(*scratch)
    _sweep(refs, qi, score, value)
    ot = jnp.concatenate([refs.acc[0] / refs.l[0], refs.acc[1] / refs.l[1]], axis=0)
    o_ref[...] = ot.T.astype(o_ref.dtype)


def _mla(q3, k3, vt):
    bsz, seq, _ = q3.shape
    tq = Q_TILE
    pairs = MLA_HEADS // 2
    return pl.pallas_call(
        _mla_kernel,
        grid=(bsz, pairs, seq // tq),
        in_specs=[
            pl.BlockSpec((None, tq, 2 * LANES), lambda b, p, i: (b, i, p)),
            pl.BlockSpec((None, seq, 2 * LANES), lambda b, p, i: (b, 0, p)),
            pl.BlockSpec((seq // KV_TILE, LANES, KV_TILE), lambda b, p, i: (b, p, 0)),
        ],
        out_specs=pl.BlockSpec((None, tq, LANES), lambda b, p, i: (b, i, p)),
        out_shape=jax.ShapeDtypeStruct((bsz, seq, pairs * LANES), _BF16),
        scratch_shapes=_attn_scratch(MLA_V),
        compiler_params=_params("parallel", "parallel", "arbitrary"),
        name="mla_attn",
    )(q3, k3, vt)


def _route(logits):
    lane = lax.broadcasted_iota(jnp.int32, logits.shape, 1).astype(_F32)
    big = 4.0 * LANES
    is_group = (lane >= N_EXPERTS) & (lane < N_EXPERTS + N_GROUPS)
    gl = jnp.where(is_group, logits, -jnp.inf)
    gmax = jnp.max(gl, axis=1, keepdims=True)
    grp = jnp.min(jnp.where(gl == gmax, lane, big), axis=1, keepdims=True) - N_EXPERTS
    p_group = 1.0 / jnp.sum(jnp.exp(gl - gmax), axis=1, keepdims=True)
    in_group = (lane >= grp * EXPERTS_PER_GROUP) & (lane < (grp + 1) * EXPERTS_PER_GROUP)
    el = jnp.where(in_group, logits, -jnp.inf)
    e1 = jnp.max(el, axis=1, keepdims=True)
    i1 = jnp.min(jnp.where(el == e1, lane, big), axis=1, keepdims=True)
    el2 = jnp.where(lane == i1, -jnp.inf, el)
    e2 = jnp.max(el2, axis=1, keepdims=True)
    i2 = jnp.min(jnp.where(el2 == e2, lane, big), axis=1, keepdims=True)
    r = jnp.exp(e2 - e1)
    w1 = p_group / (1.0 + r)
    w2 = w1 * r
    first = i1 < i2
    a = jnp.minimum(i1, i2) - grp * EXPERTS_PER_GROUP
    b = jnp.maximum(i1, i2) - grp * EXPERTS_PER_GROUP
    cls = grp * PAIRS_PER_GROUP + a * (2 * EXPERTS_PER_GROUP - 1 - a) * 0.5 + (b - a - 1.0)
    return jnp.where(first, w1, w2), jnp.where(first, w2, w1), cls


def _lane_dense(col, eye):
    t = col.shape[0]
    return jnp.sum(jnp.where(eye, col, 0.0).reshape(t // LANES, LANES, LANES), axis=1)


def _out_proj_kernel(*refs, n_mix):
    x_ref = refs[0]
    mix = refs[1:1 + 2 * n_mix]
    g_ref, wr_ref, br_ref = refs[1 + 2 * n_mix:4 + 2 * n_mix]
    x_out, row_out, cls_out, rank_out, count_out, count_ref = refs[4 + 2 * n_mix:]

    @pl.when(pl.program_id(0) == 0)
    def _():
        count_ref[...] = jnp.zeros(count_ref.shape, _F32)

    x = x_ref[...]
    for i in range(n_mix):
        x = x + _dot(mix[2 * i][...], mix[2 * i + 1][...])
    x_out[...] = x
    t = _rms(x, g_ref[...])
    w_lo, w_hi, cls = _route(_dot(t.astype(_BF16), wr_ref[...]) + br_ref[...])

    tm = x.shape[0]
    lane = lax.broadcasted_iota(jnp.int32, (tm, LANES), 1)
    row_out[:, :D_MODEL] = t
    row_out[:, D_MODEL:] = jnp.where(lane == 0, w_lo, 0.0) + jnp.where(lane == 1, w_hi, 0.0)

    onehot = jnp.where(lane.astype(_F32) == cls, 1.0, 0.0)
    earlier = (lax.broadcasted_iota(jnp.int32, (tm, tm), 1)
               < lax.broadcasted_iota(jnp.int32, (tm, tm), 0))
    before = _dot(jnp.where(earlier, 1.0, 0.0).astype(_BF16), onehot.astype(_BF16))
    rank = jnp.sum(onehot * (before + count_ref[...]), axis=1, keepdims=True)
    count_ref[...] += jnp.sum(onehot, axis=0, keepdims=True)
    count_out[...] = count_ref[...].astype(jnp.int32)
    eye = lax.broadcasted_iota(jnp.int32, (tm, LANES), 0) % LANES == lane
    cls_out[...] = _lane_dense(cls, eye).astype(jnp.int32)
    rank_out[...] = _lane_dense(rank, eye).astype(jnp.int32)


def _out_proj(x2, mixes, g, wr, br):
    n = x2.shape[0]
    tm = TOKEN_TILE
    in_specs = [pl.BlockSpec((tm, D_MODEL), lambda i: (i, 0))]
    args = [x2]
    for o, w in mixes:
        in_specs.append(pl.BlockSpec((tm, o.shape[1]), lambda i: (i, 0)))
        in_specs.append(pl.BlockSpec(w.shape, lambda i: (0, 0)))
        args += [o, w]
    for a in (g, wr, br):
        in_specs.append(pl.BlockSpec(a.shape, lambda i: (0, 0)))
        args.append(a)
    idx_spec = pl.BlockSpec((None, tm // LANES, LANES), lambda i: (i, 0, 0))
    idx_shape = jax.ShapeDtypeStruct((n // tm, tm // LANES, LANES), jnp.int32)
    return pl.pallas_call(
        functools.partial(_out_proj_kernel, n_mix=len(mixes)),
        grid=(n // tm,),
        in_specs=in_specs,
        out_specs=[pl.BlockSpec((tm, D_MODEL), lambda i: (i, 0)),
                   pl.BlockSpec((tm, ROW_W), lambda i: (i, 0)),
                   idx_spec, idx_spec,
                   pl.BlockSpec((1, LANES), lambda i: (0, 0))],
        out_shape=[jax.ShapeDtypeStruct((n, D_MODEL), _F32),
                   jax.ShapeDtypeStruct((n, ROW_W), _F32),
                   idx_shape, idx_shape,
                   jax.ShapeDtypeStruct((1, LANES), jnp.int32)],
        scratch_shapes=[pltpu.VMEM((1, LANES), _F32)],
        compiler_params=_params("arbitrary"),
        name="out_proj_route",
    )(*args)


def _class_experts():
    pairs = [(a, b) for a in range(EXPERTS_PER_GROUP) for b in range(a + 1, EXPERTS_PER_GROUP)]
    lo = [g * EXPERTS_PER_GROUP + a for g in range(N_GROUPS) for a, _ in pairs]
    hi = [g * EXPERTS_PER_GROUP + b for g in range(N_GROUPS) for _, b in pairs]
    return jnp.asarray(lo, jnp.int32), jnp.asarray(hi, jnp.int32)


def _tile_plan(counts, max_tiles):
    tiles = (counts + EXPERT_TILE - 1) // EXPERT_TILE
    ends = jnp.cumsum(tiles)
    offsets = (ends - tiles) * EXPERT_TILE
    tile = jnp.arange(max_tiles, dtype=jnp.int32)
    tile_cls = jnp.sum((ends[None, :] <= tile[:, None]).astype(jnp.int32), axis=1)
    tile_cls = jnp.minimum(tile_cls, N_CLASSES - 1)
    lo, hi = _class_experts()
    return offsets.astype(jnp.int32), lo[tile_cls], hi[tile_cls], ends[-1:].astype(jnp.int32)


def _row_copies(cls_ref, rank_ref, off_ref, copy):
    def each(method):
        for a in range(cls_ref.shape[0]):
            def body(l, carry, a=a):
                d = off_ref[cls_ref[a, l]] + rank_ref[a, l]
                getattr(copy(a * LANES + l, d), method)()
                return carry

            lax.fori_loop(0, LANES, body, 0, unroll=ROW_COPY_UNROLL)

    each("start")
    each("wait")


def _dispatch_kernel(cls_ref, rank_ref, off_ref, row_ref, xs_in_ref, xs_ref, sem):
    del xs_in_ref

    def copy(r, d):
        return pltpu.make_async_copy(row_ref.at[pl.ds(r, 1), :], xs_ref.at[pl.ds(d, 1), :], sem)

    _row_copies(cls_ref, rank_ref, off_ref, copy)


def _index_specs(tm):
    smem = pltpu.SMEM
    return [pl.BlockSpec((None, tm // LANES, LANES), lambda i: (i, 0, 0), memory_space=smem),
            pl.BlockSpec((None, tm // LANES, LANES), lambda i: (i, 0, 0), memory_space=smem),
            pl.BlockSpec(memory_space=smem)]


def _dispatch(cls, rank, offsets, rows, max_tiles):
    n = rows.shape[0]
    tm = TOKEN_TILE
    xs0 = jnp.zeros((max_tiles * EXPERT_TILE, ROW_W), _F32)
    return pl.pallas_call(
        _dispatch_kernel,
        grid=(n // tm,),
        in_specs=_index_specs(tm) + [pl.BlockSpec((tm, ROW_W), lambda i: (i, 0)),
                                     pl.BlockSpec(memory_space=pl.ANY)],
        out_specs=pl.BlockSpec(memory_space=pl.ANY),
        out_shape=jax.ShapeDtypeStruct(xs0.shape, _F32),
        scratch_shapes=[pltpu.SemaphoreType.DMA(())],
        input_output_aliases={4: 0},
        compiler_params=_params("arbitrary"),
        name="moe_dispatch",
    )(cls, rank, offsets, rows, xs0)


def _expert_kernel(lo_ref, hi_ref, used_ref, xs_ref, wg0, wu0, wd0, wg1, wu1, wd1, ys_ref):
    del lo_ref, hi_ref
    i = pl.program_id(0)

    @pl.when(i < used_ref[0])
    def _():
        xs = xs_ref[...]
        x = xs[:, :D_MODEL].astype(_BF16)
        gates = xs[:, D_MODEL:]
        y = None
        for k, (wg, wu, wd) in enumerate(((wg0, wu0, wd0), (wg1, wu1, wd1))):
            hg = _dot(x, wg[...])
            hu = _dot(x, wu[...])
            act = hg * (1.0 / (1.0 + jnp.exp(-hg))) * hu * gates[:, k:k + 1]
            yk = _dot(act.astype(_BF16), wd[...])
            y = yk if y is None else y + yk
        ys_ref[...] = y

    @pl.when(i >= used_ref[0])
    def _():
        ys_ref[...] = jnp.zeros(ys_ref.shape, _F32)


def _experts(xs, tile_lo, tile_hi, used, wg, wu, wd):
    max_tiles = xs.shape[0] // EXPERT_TILE
    up = (None, D_MODEL, EXPERT_FF)
    down = (None, EXPERT_FF, D_MODEL)
    by_lo = lambda i, lo, hi, used: (lo[i], 0, 0)
    by_hi = lambda i, lo, hi, used: (hi[i], 0, 0)
    grid_spec = pltpu.PrefetchScalarGridSpec(
        num_scalar_prefetch=3,
        grid=(max_tiles,),
        in_specs=[pl.BlockSpec((EXPERT_TILE, ROW_W), lambda i, lo, hi, used: (i, 0)),
                  pl.BlockSpec(up, by_lo), pl.BlockSpec(up, by_lo), pl.BlockSpec(down, by_lo),
                  pl.BlockSpec(up, by_hi), pl.BlockSpec(up, by_hi), pl.BlockSpec(down, by_hi)],
        out_specs=pl.BlockSpec((EXPERT_TILE, D_MODEL), lambda i, lo, hi, used: (i, 0)),
    )
    return pl.pallas_call(
        _expert_kernel,
        grid_spec=grid_spec,
        out_shape=jax.ShapeDtypeStruct((xs.shape[0], D_MODEL), _F32),
        compiler_params=_params("arbitrary"),
        name="moe_experts",
    )(tile_lo, tile_hi, used, xs, wg, wu, wd, wg, wu, wd)


def _gather_rows(cls_ref, rank_ref, off_ref, ys_ref, buf_ref, sem):
    def copy(r, d):
        return pltpu.make_async_copy(ys_ref.at[pl.ds(d, 1), :], buf_ref.at[pl.ds(r, 1), :], sem)

    _row_copies(cls_ref, rank_ref, off_ref, copy)


def _final_kernel(cls_ref, rank_ref, off_ref, x_ref, g_ref, ys_ref, o_ref, buf_ref, sem):
    _gather_rows(cls_ref, rank_ref, off_ref, ys_ref, buf_ref, sem)
    o_ref[...] = _rms(x_ref[...] + buf_ref[...], g_ref[...])


def _final(cls, rank, offsets, x2, g, ys):
    n = x2.shape[0]
    tm = TOKEN_TILE
    return pl.pallas_call(
        _final_kernel,
        grid=(n // tm,),
        in_specs=_index_specs(tm) + [pl.BlockSpec((tm, D_MODEL), lambda i: (i, 0)),
                                     pl.BlockSpec((1, D_MODEL), lambda i: (0, 0)),
                                     pl.BlockSpec(memory_space=pl.ANY)],
        out_specs=pl.BlockSpec((tm, D_MODEL), lambda i: (i, 0)),
        out_shape=jax.ShapeDtypeStruct((n, D_MODEL), _F32),
        scratch_shapes=[pltpu.VMEM((tm, D_MODEL), _F32), pltpu.SemaphoreType.DMA(())],
        compiler_params=_params("arbitrary"),
        name="moe_combine_norm",
    )(cls, rank, offsets, x2, g, ys)


def _moe(rows, cls, rank, counts, wg, wu, wd):
    n = rows.shape[0]
    max_tiles = n // EXPERT_TILE + N_CLASSES
    offsets, tile_lo, tile_hi, used = _tile_plan(counts[0], max_tiles)
    xs = _dispatch(cls, rank, offsets, rows, max_tiles)
    return _experts(xs, tile_lo, tile_hi, used, wg, wu, wd), offsets


def _router_weights(w_group, b_group, w_expert, b_expert):
    pad = LANES - N_EXPERTS - N_GROUPS
    wr = jnp.concatenate([w_expert, w_group, jnp.zeros((D_MODEL, pad), _F32)], axis=1)
    br = jnp.concatenate([b_expert, b_group, jnp.zeros((pad,), _F32)])[None, :]
    return wr.astype(_BF16), br


def _mla_weights(w_in, w_q_up, w_kv_up):
    lat = MLA_Q_RANK + MLA_KV_RANK
    w1 = jnp.concatenate([w_in[:, :lat], jnp.zeros((D_MODEL, MLA_NOPE), _F32), w_in[:, lat:],
                          jnp.zeros((D_MODEL, LANES - MLA_NOPE - MLA_ROPE), _F32)], axis=1)
    qd = MLA_NOPE + MLA_ROPE
    wq = jnp.pad(w_q_up.reshape(MLA_Q_RANK, MLA_HEADS, qd), ((0, 0), (0, 0), (0, LANES - qd)))
    kv = w_kv_up.reshape(MLA_KV_RANK, MLA_HEADS, MLA_NOPE + MLA_V)
    wk = jnp.pad(kv[:, :, :MLA_NOPE], ((0, 0), (0, 0), (0, LANES - MLA_NOPE)))
    wv = kv[:, :, MLA_NOPE:]
    return (w1.astype(_BF16), wq.reshape(MLA_Q_RANK, -1).astype(_BF16),
            wk.reshape(MLA_KV_RANK, -1).astype(_BF16), wv.reshape(MLA_KV_RANK, -1).astype(_BF16))


def kernel(x, attn_norm, ev_w_in, ev_lambda_q1, ev_lambda_k1, ev_lambda_q2, ev_lambda_k2, ev_subln, ev_w_out, od_w_in, od_q_norm, od_kv_norm, od_w_q_up, od_w_kv_up, od_w_out, ffn_norm, moe_w_group, moe_b_group, moe_w_expert, moe_b_expert, moe_w_gate, moe_w_up, moe_w_down, final_norm):
    bsz, seq, d = x.shape
    n = bsz * seq
    x2 = x.reshape(n, d)

    lam_init = 0.8 - 0.6 * math.exp(-0.3 * 0)
    qk, vt = _even_proj(x2, attn_norm[0][None, :], ev_w_in[0].astype(_BF16),
                        _even_rope_tables(seq), seq)
    qk3 = qk.reshape(bsz, seq, -1)
    o_a = _moba(qk3, vt).reshape(n, -1)
    lam_rows = jnp.stack([ev_lambda_q1[0], ev_lambda_k1[0], ev_lambda_q2[0], ev_lambda_k2[0]])
    o_b = _diff(qk3, vt, lam_rows, ev_subln[0][None, :], lam_init).reshape(n, -1)
    w_out = ev_w_out[0].astype(_BF16)
    wa, wb = w_out[:o_a.shape[1]], w_out[o_a.shape[1]:]
    wr, br = _router_weights(moe_w_group[0], moe_b_group[0], moe_w_expert[0], moe_b_expert[0])
    x2, rows, cls, rank, counts = _out_proj(x2, [(o_a, wa), (o_b, wb)], ffn_norm[0][None, :],
                                            wr, br)
    ys, offsets = _moe(rows, cls, rank, counts, moe_w_gate[0].astype(_BF16),
                       moe_w_up[0].astype(_BF16), moe_w_down[0].astype(_BF16))

    w1, wq, wk, wv = _mla_weights(od_w_in[0], od_w_q_up[0], od_w_kv_up[0])
    tabs = _mla_rope_tables(seq, (MLA_NOPE + MLA_ROPE) ** -0.5 * LOG2E)
    x2, q, k, vt = _mla_proj(cls, rank, offsets, x2, ys, attn_norm[1][None, :], w1,
                             od_q_norm[0][None, :], od_kv_norm[0][None, :], wq, wk, wv, tabs, seq)
    o_c = _mla(q.reshape(bsz, seq, -1), k.reshape(bsz, seq, -1), vt)
    wr, br = _router_weights(moe_w_group[1], moe_b_group[1], moe_w_expert[1], moe_b_expert[1])
    x2, rows, cls, rank, counts = _out_proj(x2, [(o_c.reshape(n, -1), od_w_out[0].astype(_BF16))],
                                            ffn_norm[1][None, :], wr, br)
    ys, offsets = _moe(rows, cls, rank, counts, moe_w_gate[1].astype(_BF16),
                       moe_w_up[1].astype(_BF16), moe_w_down[1].astype(_BF16))
    out = _final(cls, rank, offsets, x2, final_norm[None, :], ys)
    return out.reshape(bsz, seq, d)
```

```python
import functools
import math

import jax
import jax.numpy as jnp
from jax import lax
from jax.experimental import pallas as pl
from jax.experimental.pallas import tpu as pltpu

D_MODEL = 1024
ROPE_THETA = 10000.0
NORM_EPS = 1e-6

HEAD_DIM = 64
MOBA_HEADS = 8
MOBA_BLOCK = 256
MOBA_TOPK = 3
DIFF_HEADS = 4
EVEN_IN = 3072

MLA_HEADS = 16
MLA_Q_RANK = 256
MLA_KV_RANK = 128
MLA_NOPE = 64
MLA_ROPE = 32
MLA_V = 64

N_GROUPS = 4
EXPERTS_PER_GROUP = 8
N_EXPERTS = 32
EXPERT_FF = 256
PAIRS_PER_GROUP = EXPERTS_PER_GROUP * (EXPERTS_PER_GROUP - 1) // 2
N_CLASSES = N_GROUPS * PAIRS_PER_GROUP

LANES = 128
LOG2E = math.log2(math.e)
MASK_BIAS = -1e30
VMEM_LIMIT = 48 * 1024 * 1024

TOKEN_TILE = 512
EXPERT_TILE = 256
ROW_W = D_MODEL + 128
ROW_COPY_UNROLL = 8
Q_TILE = 512
KV_TILE = 1024
SUB_TILE = 256

_F32 = jnp.float32
_BF16 = jnp.bfloat16


def _dot(a, b):
    return jnp.dot(a, b, preferred_element_type=_F32)


def _rms(x, g):
    return x * lax.rsqrt(jnp.mean(x * x, axis=-1, keepdims=True) + NORM_EPS) * g


def _params(*sem):
    return pltpu.CompilerParams(dimension_semantics=sem, vmem_limit_bytes=VMEM_LIMIT)


def _rope_tables(seq, dim):
    inv = 1.0 / (ROPE_THETA ** (jnp.arange(0, dim, 2, dtype=_F32) / dim))
    ang = jnp.arange(seq, dtype=_F32)[:, None] * inv[None, :]
    return jnp.cos(ang), jnp.sin(ang)


def _even_rope_tables(seq):
    cos, sin = _rope_tables(seq, HEAD_DIM)
    half = HEAD_DIM // 2
    lane = jnp.arange(LANES)
    first = (lane % HEAD_DIM) < half
    c = cos[:, lane % half]
    s = sin[:, lane % half]
    return c, jnp.where(first, -s, 0.0), jnp.where(first, 0.0, s)


def _mla_rope_tables(seq, scale):
    cos, sin = _rope_tables(seq, MLA_ROPE)
    half = MLA_ROPE // 2
    lane = jnp.arange(LANES)
    r = (lane - MLA_NOPE) % half
    in_rope = (lane >= MLA_NOPE) & (lane < MLA_NOPE + MLA_ROPE)
    first = (lane >= MLA_NOPE) & (lane < MLA_NOPE + half)
    second = (lane >= MLA_NOPE + half) & (lane < MLA_NOPE + MLA_ROPE)
    c, s = cos[:, r], sin[:, r]
    cq = jnp.where(lane < MLA_NOPE, 1.0, jnp.where(in_rope, c, 0.0)) * scale
    ck = jnp.where(in_rope, c, 0.0)
    sa = jnp.where(first, -s, 0.0)
    sb = jnp.where(second, s, 0.0)
    return cq, sa * scale, sb * scale, ck, sa, sb


def _rotate(z, c, sa, sb, half):
    return z * c + pltpu.roll(z, LANES - half, 1) * sa + pltpu.roll(z, half, 1) * sb


def _store_transposed(vt_ref, row0, z):
    vt_ref[row0:row0 + LANES, :] = z.T.astype(_BF16)


def _vt_spec(features, tm):
    per_slab = KV_TILE // tm
    return pl.BlockSpec((None, features, tm), lambda i: (i // per_slab, 0, i % per_slab))


def _even_proj_kernel(x_ref, g_ref, w_ref, c_ref, sa_ref, sb_ref, qk_ref, vt_ref):
    h = _rms(x_ref[...], g_ref[...]).astype(_BF16)
    c, sa, sb = c_ref[...], sa_ref[...], sb_ref[...]
    chunk = 512
    blocks = chunk // LANES
    qk_chunk = {0: 0, 1: 1, 3: 2, 4: 3}
    v_chunk = {2: 0, 5: 1}
    for ci in range(EVEN_IN // chunk):
        z = _dot(h, w_ref[:, ci * chunk:(ci + 1) * chunk])
        for b in range(blocks):
            zz = z[:, b * LANES:(b + 1) * LANES]
            if ci in v_chunk:
                _store_transposed(vt_ref, (v_chunk[ci] * blocks + b) * LANES, zz)
                continue
            zz = _rotate(zz, c, sa, sb, HEAD_DIM // 2)
            if ci in (0, 3):
                zz = zz * (HEAD_DIM ** -0.5 * LOG2E)
            col = (qk_chunk[ci] * blocks + b) * LANES
            qk_ref[:, col:col + LANES] = zz.astype(_BF16)


def _even_proj(x2, g, w, tabs, seq):
    n = x2.shape[0]
    tm = TOKEN_TILE
    spb = seq // tm
    tab_spec = pl.BlockSpec((tm, LANES), lambda i: (i % spb, 0))
    qk_w = 4 * 512
    v_w = 2 * 512
    return pl.pallas_call(
        _even_proj_kernel,
        grid=(n // tm,),
        in_specs=[
            pl.BlockSpec((tm, D_MODEL), lambda i: (i, 0)),
            pl.BlockSpec((1, D_MODEL), lambda i: (0, 0)),
            pl.BlockSpec((D_MODEL, EVEN_IN), lambda i: (0, 0)),
            tab_spec, tab_spec, tab_spec,
        ],
        out_specs=[pl.BlockSpec((tm, qk_w), lambda i: (i, 0)), _vt_spec(v_w, tm)],
        out_shape=[jax.ShapeDtypeStruct((n, qk_w), _BF16),
                   jax.ShapeDtypeStruct((n // KV_TILE, v_w, KV_TILE), _BF16)],
        compiler_params=_params("parallel"),
        name="even_proj",
    )(x2, g, w, *tabs)


class _AttnRefs:
    def __init__(self, s_ref, pm_ref, m_ref, l_ref, acc_ref):
        self.s, self.pm, self.m, self.l, self.acc = s_ref, pm_ref, m_ref, l_ref, acc_ref


def _attn_scratch(dv, tq):
    return [pltpu.VMEM((2, 2, KV_TILE, tq), _F32),
            pltpu.VMEM((2, 2, 8, tq), _F32),
            pltpu.VMEM((2, 1, tq), _F32),
            pltpu.VMEM((2, 1, tq), _F32),
            pltpu.VMEM((2, dv, tq), _F32)]


def _kv_rows(c):
    return pl.ds(pl.multiple_of(c * KV_TILE, KV_TILE), KV_TILE)


def _sub_rows(c, r):
    return pl.ds(pl.multiple_of(c * KV_TILE + r * SUB_TILE, SUB_TILE), SUB_TILE)


def _fold8(x, op):
    return op(x.reshape(x.shape[0] // 8, 8, x.shape[1]), axis=0)


def _stage(refs, score, value, scores_of=None, softmax_of=None, mask=None):
    chains = range(2)
    if softmax_of is not None:
        slot, c = softmax_of
        m_new, alpha, acc, lsum = [], [], [], [None, None]
        for t in chains:
            m_prev = refs.m[t]
            m_new.append(jnp.maximum(m_prev, jnp.max(refs.pm[slot, t], axis=0, keepdims=True)))
            alpha.append(jnp.exp2(m_prev - m_new[t]))
            acc.append(alpha[t] * refs.acc[t])
    if scores_of is not None:
        slot_out, c_out = scores_of
        pm = [None, None]
    for r in range(KV_TILE // SUB_TILE):
        rows = slice(r * SUB_TILE, (r + 1) * SUB_TILE)
        for t in chains:
            if scores_of is not None:
                s = score(t, c_out, r)
                if mask is not None:
                    s = mask(s, c_out, r)
                refs.s[slot_out, t, rows, :] = s
                part = _fold8(s, jnp.max)
                pm[t] = part if pm[t] is None else jnp.maximum(pm[t], part)
            if softmax_of is not None:
                p = jnp.exp2(refs.s[slot, t, rows, :] - m_new[t])
                part = _fold8(p, jnp.sum)
                lsum[t] = part if lsum[t] is None else lsum[t] + part
                acc[t] = acc[t] + _dot(value(t, c, r), p.astype(_BF16))
    for t in chains:
        if scores_of is not None:
            refs.pm[slot_out, t] = pm[t]
        if softmax_of is not None:
            refs.m[t] = m_new[t]
            refs.l[t] = alpha[t] * refs.l[t] + jnp.sum(lsum[t], axis=0, keepdims=True)
            refs.acc[t] = acc[t]


def _sweep(refs, qi, score, value):
    tq = refs.m.shape[-1]
    last = (qi * tq) // KV_TILE
    refs.m[...] = jnp.full(refs.m.shape, -jnp.inf, _F32)
    refs.l[...] = jnp.zeros(refs.l.shape, _F32)
    refs.acc[...] = jnp.zeros(refs.acc.shape, _F32)

    def causal(s, c, r):
        key = lax.broadcasted_iota(jnp.int32, s.shape, 0) + (c * KV_TILE + r * SUB_TILE)
        qry = lax.broadcasted_iota(jnp.int32, s.shape, 1) + qi * tq
        return jnp.where(key <= qry, s, -jnp.inf)

    _stage(refs, score, value, scores_of=(0, last), mask=causal)

    def pair(k, carry):
        i = 2 * k
        _stage(refs, score, value, softmax_of=(0, jnp.where(k == 0, last, i - 1)), scores_of=(1, i))
        _stage(refs, score, value, softmax_of=(1, i), scores_of=(0, i + 1))
        return carry

    lax.fori_loop(0, last // 2, pair, 0)

    @pl.when(last % 2 == 1)
    def _():
        _stage(refs, score, value, softmax_of=(0, jnp.where(last == 1, last, last - 2)),
               scores_of=(1, last - 1))

    _stage(refs, score, value, softmax_of=(last % 2, jnp.maximum(last - 1, 0)))


def _transposed(q):
    return q.astype(_F32).T


def _moba_kernel(q_ref, k_ref, vt_ref, o_ref, km_ref, ka_ref, *scratch, nb, topk):
    qi = pl.program_id(2)
    tq = q_ref.shape[0]
    half = HEAD_DIM // 2

    @pl.when(qi == 0)
    def _():
        kf = k_ref[...].astype(_F32)
        means = jnp.sum(kf.reshape(nb, MOBA_BLOCK, LANES), axis=1) * (1.0 / MOBA_BLOCK)
        lane_b = lax.broadcasted_iota(jnp.int32, (nb, LANES), 1)
        km_ref[...] = jnp.zeros(km_ref.shape, _F32)
        km_ref[0:nb, :] = jnp.where(lane_b < HEAD_DIM, 0.0, means)
        km_ref[HEAD_DIM:HEAD_DIM + nb, :] = jnp.where(lane_b < HEAD_DIM, means, 0.0)

        lane = lax.broadcasted_iota(jnp.int32, (KV_TILE, LANES), 1)
        blk = lax.broadcasted_iota(jnp.int32, (KV_TILE, LANES), 0) // MOBA_BLOCK

        def build(c, carry):
            rows = _kv_rows(c)
            k = k_ref[rows, :].astype(_F32)
            b = blk + c * (KV_TILE // MOBA_BLOCK)
            ka_ref[0, rows, :] = jnp.where(lane < HEAD_DIM, k,
                                           jnp.where(lane == b + HEAD_DIM, 1.0, 0.0)).astype(_BF16)
            ka_ref[1, rows, :] = jnp.where(lane < HEAD_DIM,
                                           jnp.where(lane == b, 1.0, 0.0), k).astype(_BF16)
            return carry

        lax.fori_loop(0, k_ref.shape[0] // KV_TILE, build, 0)

    qt = _transposed(q_ref[...])
    gate = _dot(km_ref[...].astype(_BF16), qt.astype(_BF16))
    row = lax.broadcasted_iota(jnp.int32, (half, tq), 0).astype(_F32)
    qpos = qi * tq + lax.broadcasted_iota(jnp.int32, (half, tq), 1)
    own = (qpos // MOBA_BLOCK).astype(_F32)

    def block_bias(base):
        g = jnp.where(row < own, gate[base:base + half], -jnp.inf)
        bias = jnp.where(row == own, 0.0, MASK_BIAS)
        for _ in range(topk):
            mx = jnp.max(g, axis=0, keepdims=True)
            hit = (g == mx) & (mx > -jnp.inf)
            idx = jnp.min(jnp.where(hit, row, 4.0 * LANES), axis=0, keepdims=True)
            pick = row == idx
            bias = jnp.where(pick, 0.0, bias)
            g = jnp.where(pick, -jnp.inf, g)
        return bias

    z_half = jnp.zeros((half, tq), _F32)
    qs = (jnp.concatenate([qt[:HEAD_DIM], block_bias(HEAD_DIM), z_half], axis=0).astype(_BF16),
          jnp.concatenate([block_bias(0), z_half, qt[HEAD_DIM:]], axis=0).astype(_BF16))

    def score(t, c, r):
        return _dot(ka_ref[t, _sub_rows(c, r), :], qs[t])

    def value(t, c, r):
        return vt_ref[c, t * HEAD_DIM:(t + 1) * HEAD_DIM, r * SUB_TILE:(r + 1) * SUB_TILE]

    refs = _AttnRefs(*scratch)
    _sweep(refs, qi, score, value)
    ot = jnp.concatenate([refs.acc[0] / refs.l[0], refs.acc[1] / refs.l[1]], axis=0)
    o_ref[...] = ot.T.astype(o_ref.dtype)


def _moba(qk3, vt):
    bsz, seq, _ = qk3.shape
    tq = Q_TILE
    assert tq % MOBA_BLOCK == 0 and seq % KV_TILE == 0
    nb = seq // MOBA_BLOCK
    assert nb <= HEAD_DIM // 2
    topk = max(1, min(MOBA_TOPK, nb - 1))
    pairs = MOBA_HEADS // 2
    return pl.pallas_call(
        functools.partial(_moba_kernel, nb=nb, topk=topk),
        grid=(bsz, pairs, seq // tq),
        in_specs=[
            pl.BlockSpec((None, tq, LANES), lambda b, p, i: (b, i, p)),
            pl.BlockSpec((None, seq, LANES), lambda b, p, i: (b, 0, pairs + p)),
            pl.BlockSpec((seq // KV_TILE, LANES, KV_TILE), lambda b, p, i: (b, p, 0)),
        ],
        out_specs=pl.BlockSpec((None, tq, LANES), lambda b, p, i: (b, i, p)),
        out_shape=jax.ShapeDtypeStruct((bsz, seq, pairs * LANES), _BF16),
        scratch_shapes=[pltpu.VMEM((LANES, LANES), _F32), pltpu.VMEM((2, seq, LANES), _BF16)]
                       + _attn_scratch(HEAD_DIM, tq),
        compiler_params=_params("parallel", "parallel", "arbitrary"),
        name="moba_attn",
    )(qk3, qk3, vt)


def _diff_kernel(q_ref, k_ref, vt_ref, lam_ref, g_ref, o_ref, *scratch, lam_init):
    qi = pl.program_id(2)
    tq = q_ref.shape[0]
    qt = _transposed(q_ref[...])
    z_head = jnp.zeros((HEAD_DIM, tq), _F32)
    qd = (jnp.concatenate([qt[:HEAD_DIM], z_head], axis=0).astype(_BF16),
          jnp.concatenate([z_head, qt[HEAD_DIM:]], axis=0).astype(_BF16))

    def score(t, c, r):
        return _dot(k_ref[_sub_rows(c, r), :], qd[t])

    def value(t, c, r):
        return vt_ref[c, :, r * SUB_TILE:(r + 1) * SUB_TILE]

    refs = _AttnRefs(*scratch)
    _sweep(refs, qi, score, value)

    lv = lam_ref[...]
    lam = (jnp.exp(jnp.sum(lv[0:1] * lv[1:2], axis=1, keepdims=True))
           - jnp.exp(jnp.sum(lv[2:3] * lv[3:4], axis=1, keepdims=True)) + lam_init)
    ot = refs.acc[0] / refs.l[0] - lam * (refs.acc[1] / refs.l[1])
    o_ref[...] = (_rms(ot.T, g_ref[...]) * (1.0 - lam_init)).astype(o_ref.dtype)


def _diff(qk3, vt, lam_rows, subln, lam_init):
    bsz, seq, _ = qk3.shape
    tq = Q_TILE
    base = 2 * MOBA_HEADS * HEAD_DIM // LANES
    vbase = MOBA_HEADS * HEAD_DIM // LANES
    return pl.pallas_call(
        functools.partial(_diff_kernel, lam_init=lam_init),
        grid=(bsz, DIFF_HEADS, seq // tq),
        in_specs=[
            pl.BlockSpec((None, tq, LANES), lambda b, h, i: (b, i, base + h)),
            pl.BlockSpec((None, seq, LANES), lambda b, h, i: (b, 0, base + DIFF_HEADS + h)),
            pl.BlockSpec((seq // KV_TILE, LANES, KV_TILE), lambda b, h, i: (b, vbase + h, 0)),
            pl.BlockSpec((4, HEAD_DIM), lambda b, h, i: (0, 0)),
            pl.BlockSpec((1, LANES), lambda b, h, i: (0, 0)),
        ],
        out_specs=pl.BlockSpec((None, tq, LANES), lambda b, h, i: (b, i, h)),
        out_shape=jax.ShapeDtypeStruct((bsz, seq, DIFF_HEADS * LANES), _BF16),
        scratch_shapes=_attn_scratch(LANES, tq),
        compiler_params=_params("parallel", "parallel", "arbitrary"),
        name="diff_attn",
    )(qk3, qk3, vt, lam_rows, subln)


def _mla_proj_kernel(cls_ref, rank_ref, off_ref, x_ref, ys_ref,
                     g_ref, w1_ref, qg_ref, kg_ref, wq_ref, wk_ref, wv_ref,
                     cq_ref, saq_ref, sbq_ref, ck_ref, sak_ref, sbk_ref,
                     x_out, q_out, k_out, vt_out, buf_ref, sem):
    _gather_rows(cls_ref, rank_ref, off_ref, ys_ref, buf_ref, sem)
    x = x_ref[...] + buf_ref[...]
    x_out[...] = x
    h = _rms(x, g_ref[...]).astype(_BF16)
    z = _dot(h, w1_ref[...])
    cq = _rms(z[:, :MLA_Q_RANK], qg_ref[...]).astype(_BF16)
    ckv = _rms(z[:, MLA_Q_RANK:MLA_Q_RANK + MLA_KV_RANK], kg_ref[...]).astype(_BF16)
    kr = z[:, MLA_Q_RANK + MLA_KV_RANK:]
    half = MLA_ROPE // 2
    kr = _rotate(kr, ck_ref[...], sak_ref[...], sbk_ref[...], half)
    qf = _dot(cq, wq_ref[...])
    kf = _dot(ckv, wk_ref[...])
    cqt, saq, sbq = cq_ref[...], saq_ref[...], sbq_ref[...]
    for hd in range(MLA_HEADS):
        sl = slice(hd * LANES, (hd + 1) * LANES)
        q_out[:, sl] = _rotate(qf[:, sl], cqt, saq, sbq, half).astype(_BF16)
        k_out[:, sl] = (kf[:, sl] + kr).astype(_BF16)
    vf = _dot(ckv, wv_ref[...])
    for b in range(MLA_HEADS * MLA_V // LANES):
        _store_transposed(vt_out, b * LANES, vf[:, b * LANES:(b + 1) * LANES])


def _mla_proj(cls, rank, offsets, x2, ys, g, w1, qg, kg, wq, wk, wv, tabs, seq):
    n = x2.shape[0]
    tm = TOKEN_TILE
    spb = seq // tm
    hw = MLA_HEADS * LANES
    vw = MLA_HEADS * MLA_V
    tab_spec = pl.BlockSpec((tm, LANES), lambda i: (i % spb, 0))

    def full(a):
        return pl.BlockSpec(a.shape, lambda i: (0,) * a.ndim)

    return pl.pallas_call(
        _mla_proj_kernel,
        grid=(n // tm,),
        in_specs=_index_specs(tm)
                 + [pl.BlockSpec((tm, D_MODEL), lambda i: (i, 0)),
                    pl.BlockSpec(memory_space=pl.ANY),
                    full(g), full(w1), full(qg), full(kg), full(wq), full(wk), full(wv)]
                 + [tab_spec] * 6,
        out_specs=[pl.BlockSpec((tm, D_MODEL), lambda i: (i, 0)),
                   pl.BlockSpec((tm, hw), lambda i: (i, 0)),
                   pl.BlockSpec((tm, hw), lambda i: (i, 0)),
                   _vt_spec(vw, tm)],
        out_shape=[jax.ShapeDtypeStruct((n, D_MODEL), _F32),
                   jax.ShapeDtypeStruct((n, hw), _BF16),
                   jax.ShapeDtypeStruct((n, hw), _BF16),
                   jax.ShapeDtypeStruct((n // KV_TILE, vw, KV_TILE), _BF16)],
        scratch_shapes=[pltpu.VMEM((tm, D_MODEL), _F32), pltpu.SemaphoreType.DMA(())],
        compiler_params=_params("arbitrary"),
        name="mla_proj",
    )(cls, rank, offsets, x2, ys, g, w1, qg, kg, wq, wk, wv, *tabs)


def _mla_kernel(q_ref, k_ref, vt_ref, o_ref, *scratch):
    qi = pl.program_id(2)
    q = q_ref[...]
    qd = tuple(_transposed(q[:, t * LANES:(t + 1) * LANES]).astype(_BF16) for t in range(2))

    def score(t, c, r):
        return _dot(k_ref[_sub_rows(c, r), t * LANES:(t + 1) * LANES], qd[t])

    def value(t, c, r):
        return vt_ref[c, t * MLA_V:(t + 1) * MLA_V, r * SUB_TILE:(r + 1) * SUB_TILE]

    refs = _AttnRefs(*scratch)
    _sweep(refs, qi, score, value)
    ot = jnp.concatenate([refs.acc[0] / refs.l[0], refs.acc[1] / refs.l[1]], axis=0)
    o_ref[...] = ot.T.astype(o_ref.dtype)


def _mla(q3, k3, vt):
    bsz, seq, _ = q3.shape
    tq = Q_TILE
    pairs = MLA_HEADS // 2
    return pl.pallas_call(
        _mla_kernel,
        grid=(bsz, pairs, seq // tq),
        in_specs=[
            pl.BlockSpec((None, tq, 2 * LANES), lambda b, p, i: (b, i, p)),
            pl.BlockSpec((None, seq, 2 * LANES), lambda b, p, i: (b, 0, p)),
            pl.BlockSpec((seq // KV_TILE, LANES, KV_TILE), lambda b, p, i: (b, p, 0)),
        ],
        out_specs=pl.BlockSpec((None, tq, LANES), lambda b, p, i: (b, i, p)),
        out_shape=jax.ShapeDtypeStruct((bsz, seq, pairs * LANES), _BF16),
        scratch_shapes=_attn_scratch(MLA_V, tq),
        compiler_params=_params("parallel", "parallel", "arbitrary"),
        name="mla_attn",
    )(q3, k3, vt)


def _route(logits):
    lane = lax.broadcasted_iota(jnp.int32, logits.shape, 1).astype(_F32)
    big = 4.0 * LANES
    is_group = (lane >= N_EXPERTS) & (lane < N_EXPERTS + N_GROUPS)
    gl = jnp.where(is_group, logits, -jnp.inf)
    gmax = jnp.max(gl, axis=1, keepdims=True)
    grp = jnp.min(jnp.where(gl == gmax, lane, big), axis=1, keepdims=True) - N_EXPERTS
    p_group = 1.0 / jnp.sum(jnp.exp(gl - gmax), axis=1, keepdims=True)
    in_group = (lane >= grp * EXPERTS_PER_GROUP) & (lane < (grp + 1) * EXPERTS_PER_GROUP)
    el = jnp.where(in_group, logits, -jnp.inf)
    e1 = jnp.max(el, axis=1, keepdims=True)
    i1 = jnp.min(jnp.where(el == e1, lane, big), axis=1, keepdims=True)
    el2 = jnp.where(lane == i1, -jnp.inf, el)
    e2 = jnp.max(el2, axis=1, keepdims=True)
    i2 = jnp.min(jnp.where(el2 == e2, lane, big), axis=1, keepdims=True)
    r = jnp.exp(e2 - e1)
    w1 = p_group / (1.0 + r)
    w2 = w1 * r
    first = i1 < i2
    a = jnp.minimum(i1, i2) - grp * EXPERTS_PER_GROUP
    b = jnp.maximum(i1, i2) - grp * EXPERTS_PER_GROUP
    cls = grp * PAIRS_PER_GROUP + a * (2 * EXPERTS_PER_GROUP - 1 - a) * 0.5 + (b - a - 1.0)
    return jnp.where(first, w1, w2), jnp.where(first, w2, w1), cls


def _lane_dense(col, eye):
    t = col.shape[0]
    return jnp.sum(jnp.where(eye, col, 0.0).reshape(t // LANES, LANES, LANES), axis=1)


def _out_proj_kernel(*refs, n_mix):
    x_ref = refs[0]
    mix = refs[1:1 + 2 * n_mix]
    g_ref, wr_ref, br_ref = refs[1 + 2 * n_mix:4 + 2 * n_mix]
    x_out, row_out, cls_out, rank_out, count_out, count_ref = refs[4 + 2 * n_mix:]

    @pl.when(pl.program_id(0) == 0)
    def _():
        count_ref[...] = jnp.zeros(count_ref.shape, _F32)

    x = x_ref[...]
    for i in range(n_mix):
        x = x + _dot(mix[2 * i][...], mix[2 * i + 1][...])
    x_out[...] = x
    t = _rms(x, g_ref[...])
    w_lo, w_hi, cls = _route(_dot(t.astype(_BF16), wr_ref[...]) + br_ref[...])

    tm = x.shape[0]
    lane = lax.broadcasted_iota(jnp.int32, (tm, LANES), 1)
    row_out[:, :D_MODEL] = t
    row_out[:, D_MODEL:] = jnp.where(lane == 0, w_lo, 0.0) + jnp.where(lane == 1, w_hi, 0.0)

    onehot = jnp.where(lane.astype(_F32) == cls, 1.0, 0.0)
    earlier = (lax.broadcasted_iota(jnp.int32, (tm, tm), 1)
               < lax.broadcasted_iota(jnp.int32, (tm, tm), 0))
    before = _dot(jnp.where(earlier, 1.0, 0.0).astype(_BF16), onehot.astype(_BF16))
    rank = jnp.sum(onehot * (before + count_ref[...]), axis=1, keepdims=True)
    count_ref[...] += jnp.sum(onehot, axis=0, keepdims=True)
    count_out[...] = count_ref[...].astype(jnp.int32)
    eye = lax.broadcasted_iota(jnp.int32, (tm, LANES), 0) % LANES == lane
    cls_out[...] = _lane_dense(cls, eye).astype(jnp.int32)
    rank_out[...] = _lane_dense(rank, eye).astype(jnp.int32)


def _out_proj(x2, mixes, g, wr, br):
    n = x2.shape[0]
    tm = TOKEN_TILE
    in_specs = [pl.BlockSpec((tm, D_MODEL), lambda i: (i, 0))]
    args = [x2]
    for o, w in mixes:
        in_specs.append(pl.BlockSpec((tm, o.shape[1]), lambda i: (i, 0)))
        in_specs.append(pl.BlockSpec(w.shape, lambda i: (0, 0)))
        args += [o, w]
    for a in (g, wr, br):
        in_specs.append(pl.BlockSpec(a.shape, lambda i: (0, 0)))
        args.append(a)
    idx_spec = pl.BlockSpec((None, tm // LANES, LANES), lambda i: (i, 0, 0))
    idx_shape = jax.ShapeDtypeStruct((n // tm, tm // LANES, LANES), jnp.int32)
    return pl.pallas_call(
        functools.partial(_out_proj_kernel, n_mix=len(mixes)),
        grid=(n // tm,),
        in_specs=in_specs,
        out_specs=[pl.BlockSpec((tm, D_MODEL), lambda i: (i, 0)),
                   pl.BlockSpec((tm, ROW_W), lambda i: (i, 0)),
                   idx_spec, idx_spec,
                   pl.BlockSpec((1, LANES), lambda i: (0, 0))],
        out_shape=[jax.ShapeDtypeStruct((n, D_MODEL), _F32),
                   jax.ShapeDtypeStruct((n, ROW_W), _F32),
                   idx_shape, idx_shape,
                   jax.ShapeDtypeStruct((1, LANES), jnp.int32)],
        scratch_shapes=[pltpu.VMEM((1, LANES), _F32)],
        compiler_params=_params("arbitrary"),
        name="out_proj_route",
    )(*args)


def _class_experts():
    pairs = [(a, b) for a in range(EXPERTS_PER_GROUP) for b in range(a + 1, EXPERTS_PER_GROUP)]
    lo = [g * EXPERTS_PER_GROUP + a for g in range(N_GROUPS) for a, _ in pairs]
    hi = [g * EXPERTS_PER_GROUP + b for g in range(N_GROUPS) for _, b in pairs]
    return jnp.asarray(lo, jnp.int32), jnp.asarray(hi, jnp.int32)


def _tile_plan(counts, max_tiles):
    tiles = (counts + EXPERT_TILE - 1) // EXPERT_TILE
    ends = jnp.cumsum(tiles)
    offsets = (ends - tiles) * EXPERT_TILE
    tile = jnp.arange(max_tiles, dtype=jnp.int32)
    tile_cls = jnp.sum((ends[None, :] <= tile[:, None]).astype(jnp.int32), axis=1)
    tile_cls = jnp.minimum(tile_cls, N_CLASSES - 1)
    lo, hi = _class_experts()
    return offsets.astype(jnp.int32), lo[tile_cls], hi[tile_cls], ends[-1:].astype(jnp.int32)


def _row_copies(cls_ref, rank_ref, off_ref, copy):
    def each(method):
        for a in range(cls_ref.shape[0]):
            def body(l, carry, a=a):
                d = off_ref[cls_ref[a, l]] + rank_ref[a, l]
                getattr(copy(a * LANES + l, d), method)()
                return carry

            lax.fori_loop(0, LANES, body, 0, unroll=ROW_COPY_UNROLL)

    each("start")
    each("wait")


def _dispatch_kernel(cls_ref, rank_ref, off_ref, row_ref, xs_in_ref, xs_ref, sem):
    del xs_in_ref

    def copy(r, d):
        return pltpu.make_async_copy(row_ref.at[pl.ds(r, 1), :], xs_ref.at[pl.ds(d, 1), :], sem)

    _row_copies(cls_ref, rank_ref, off_ref, copy)


def _index_specs(tm):
    smem = pltpu.SMEM
    return [pl.BlockSpec((None, tm // LANES, LANES), lambda i: (i, 0, 0), memory_space=smem),
            pl.BlockSpec((None, tm // LANES, LANES), lambda i: (i, 0, 0), memory_space=smem),
            pl.BlockSpec(memory_space=smem)]


def _dispatch(cls, rank, offsets, rows, max_tiles):
    n = rows.shape[0]
    tm = TOKEN_TILE
    xs0 = jnp.zeros((max_tiles * EXPERT_TILE, ROW_W), _F32)
    return pl.pallas_call(
        _dispatch_kernel,
        grid=(n // tm,),
        in_specs=_index_specs(tm) + [pl.BlockSpec((tm, ROW_W), lambda i: (i, 0)),
                                     pl.BlockSpec(memory_space=pl.ANY)],
        out_specs=pl.BlockSpec(memory_space=pl.ANY),
        out_shape=jax.ShapeDtypeStruct(xs0.shape, _F32),
        scratch_shapes=[pltpu.SemaphoreType.DMA(())],
        input_output_aliases={4: 0},
        compiler_params=_params("arbitrary"),
        name="moe_dispatch",
    )(cls, rank, offsets, rows, xs0)


def _expert_kernel(lo_ref, hi_ref, used_ref, xs_ref, wg0, wu0, wd0, wg1, wu1, wd1, ys_ref):
    del lo_ref, hi_ref
    i = pl.program_id(0)

    @pl.when(i < used_ref[0])
    def _():
        xs = xs_ref[...]
        x = xs[:, :D_MODEL].astype(_BF16)
        gates = xs[:, D_MODEL:]
        y = None
        for k, (wg, wu, wd) in enumerate(((wg0, wu0, wd0), (wg1, wu1, wd1))):
            hg = _dot(x, wg[...])
            hu = _dot(x, wu[...])
            act = hg * (1.0 / (1.0 + jnp.exp(-hg))) * hu * gates[:, k:k + 1]
            yk = _dot(act.astype(_BF16), wd[...])
            y = yk if y is None else y + yk
        ys_ref[...] = y

    @pl.when(i >= used_ref[0])
    def _():
        ys_ref[...] = jnp.zeros(ys_ref.shape, _F32)


def _experts(xs, tile_lo, tile_hi, used, wg, wu, wd):
    max_tiles = xs.shape[0] // EXPERT_TILE
    up = (None, D_MODEL, EXPERT_FF)
    down = (None, EXPERT_FF, D_MODEL)
    by_lo = lambda i, lo, hi, used: (lo[i], 0, 0)
    by_hi = lambda i, lo, hi, used: (hi[i], 0, 0)
    grid_spec = pltpu.PrefetchScalarGridSpec(
        num_scalar_prefetch=3,
        grid=(max_tiles,),
        in_specs=[pl.BlockSpec((EXPERT_TILE, ROW_W), lambda i, lo, hi, used: (i, 0)),
                  pl.BlockSpec(up, by_lo), pl.BlockSpec(up, by_lo), pl.BlockSpec(down, by_lo),
                  pl.BlockSpec(up, by_hi), pl.BlockSpec(up, by_hi), pl.BlockSpec(down, by_hi)],
        out_specs=pl.BlockSpec((EXPERT_TILE, D_MODEL), lambda i, lo, hi, used: (i, 0)),
    )
    return pl.pallas_call(
        _expert_kernel,
        grid_spec=grid_spec,
        out_shape=jax.ShapeDtypeStruct((xs.shape[0], D_MODEL), _F32),
        compiler_params=_params("arbitrary"),
        name="moe_experts",
    )(tile_lo, tile_hi, used, xs, wg, wu, wd, wg, wu, wd)


def _gather_rows(cls_ref, rank_ref, off_ref, ys_ref, buf_ref, sem):
    def copy(r, d):
        return pltpu.make_async_copy(ys_ref.at[pl.ds(d, 1), :], buf_ref.at[pl.ds(r, 1), :], sem)

    _row_copies(cls_ref, rank_ref, off_ref, copy)


def _final_kernel(cls_ref, rank_ref, off_ref, x_ref, g_ref, ys_ref, o_ref, buf_ref, sem):
    _gather_rows(cls_ref, rank_ref, off_ref, ys_ref, buf_ref, sem)
    o_ref[...] = _rms(x_ref[...] + buf_ref[...], g_ref[...])


def _final(cls, rank, offsets, x2, g, ys):
    n = x2.shape[0]
    tm = TOKEN_TILE
    return pl.pallas_call(
        _final_kernel,
        grid=(n // tm,),
        in_specs=_index_specs(tm) + [pl.BlockSpec((tm, D_MODEL), lambda i: (i, 0)),
                                     pl.BlockSpec((1, D_MODEL), lambda i: (0, 0)),
                                     pl.BlockSpec(memory_space=pl.ANY)],
        out_specs=pl.BlockSpec((tm, D_MODEL), lambda i: (i, 0)),
        out_shape=jax.ShapeDtypeStruct((n, D_MODEL), _F32),
        scratch_shapes=[pltpu.VMEM((tm, D_MODEL), _F32), pltpu.SemaphoreType.DMA(())],
        compiler_params=_params("arbitrary"),
        name="moe_combine_norm",
    )(cls, rank, offsets, x2, g, ys)


def _moe(rows, cls, rank, counts, wg, wu, wd):
    n = rows.shape[0]
    max_tiles = n // EXPERT_TILE + N_CLASSES
    offsets, tile_lo, tile_hi, used = _tile_plan(counts[0], max_tiles)
    xs = _dispatch(cls, rank, offsets, rows, max_tiles)
    return _experts(xs, tile_lo, tile_hi, used, wg, wu, wd), offsets


def _router_weights(w_group, b_group, w_expert, b_expert):
    pad = LANES - N_EXPERTS - N_GROUPS
    wr = jnp.concatenate([w_expert, w_group, jnp.zeros((D_MODEL, pad), _F32)], axis=1)
    br = jnp.concatenate([b_expert, b_group, jnp.zeros((pad,), _F32)])[None, :]
    return wr.astype(_BF16), br


def _mla_weights(w_in, w_q_up, w_kv_up):
    lat = MLA_Q_RANK + MLA_KV_RANK
    w1 = jnp.concatenate([w_in[:, :lat], jnp.zeros((D_MODEL, MLA_NOPE), _F32), w_in[:, lat:],
                          jnp.zeros((D_MODEL, LANES - MLA_NOPE - MLA_ROPE), _F32)], axis=1)
    qd = MLA_NOPE + MLA_ROPE
    wq = jnp.pad(w_q_up.reshape(MLA_Q_RANK, MLA_HEADS, qd), ((0, 0), (0, 0), (0, LANES - qd)))
    kv = w_kv_up.reshape(MLA_KV_RANK, MLA_HEADS, MLA_NOPE + MLA_V)
    wk = jnp.pad(kv[:, :, :MLA_NOPE], ((0, 0), (0, 0), (0, LANES - MLA_NOPE)))
    wv = kv[:, :, MLA_NOPE:]
    return (w1.astype(_BF16), wq.reshape(MLA_Q_RANK, -1).astype(_BF16),
            wk.reshape(MLA_KV_RANK, -1).astype(_BF16), wv.reshape(MLA_KV_RANK, -1).astype(_BF16))


def kernel(x, attn_norm, ev_w_in, ev_lambda_q1, ev_lambda_k1, ev_lambda_q2, ev_lambda_k2, ev_subln, ev_w_out, od_w_in, od_q_norm, od_kv_norm, od_w_q_up, od_w_kv_up, od_w_out, ffn_norm, moe_w_group, moe_b_group, moe_w_expert, moe_b_expert, moe_w_gate, moe_w_up, moe_w_down, final_norm):
    bsz, seq, d = x.shape
    n = bsz * seq
    x2 = x.reshape(n, d)

    lam_init = 0.8 - 0.6 * math.exp(-0.3 * 0)
    qk, vt = _even_proj(x2, attn_norm[0][None, :], ev_w_in[0].astype(_BF16),
                        _even_rope_tables(seq), seq)
    qk3 = qk.reshape(bsz, seq, -1)
    o_a = _moba(qk3, vt).reshape(n, -1)
    lam_rows = jnp.stack([ev_lambda_q1[0], ev_lambda_k1[0], ev_lambda_q2[0], ev_lambda_k2[0]])
    o_b = _diff(qk3, vt, lam_rows, ev_subln[0][None, :], lam_init).reshape(n, -1)
    w_out = ev_w_out[0].astype(_BF16)
    wa, wb = w_out[:o_a.shape[1]], w_out[o_a.shape[1]:]
    wr, br = _router_weights(moe_w_group[0], moe_b_group[0], moe_w_expert[0], moe_b_expert[0])
    x2, rows, cls, rank, counts = _out_proj(x2, [(o_a, wa), (o_b, wb)], ffn_norm[0][None, :],
                                            wr, br)
    ys, offsets = _moe(rows, cls, rank, counts, moe_w_gate[0].astype(_BF16),
                       moe_w_up[0].astype(_BF16), moe_w_down[0].astype(_BF16))

    w1, wq, wk, wv = _mla_weights(od_w_in[0], od_w_q_up[0], od_w_kv_up[0])
    tabs = _mla_rope_tables(seq, (MLA_NOPE + MLA_ROPE) ** -0.5 * LOG2E)
    x2, q, k, vt = _mla_proj(cls, rank, offsets, x2, ys, attn_norm[1][None, :], w1,
                             od_q_norm[0][None, :], od_kv_norm[0][None, :], wq, wk, wv, tabs, seq)
    o_c = _mla(q.reshape(bsz, seq, -1), k.reshape(bsz, seq, -1), vt)
    wr, br = _router_weights(moe_w_group[1], moe_b_group[1], moe_w_expert[1], moe_b_expert[1])
    x2, rows, cls, rank, counts = _out_proj(x2, [(o_c.reshape(n, -1), od_w_out[0].astype(_BF16))],
                                            ffn_norm[1][None, :], wr, br)
    ys, offsets = _moe(rows, cls, rank, counts, moe_w_gate[1].astype(_BF16),
                       moe_w_up[1].astype(_BF16), moe_w_down[1].astype(_BF16))
    out = _final(cls, rank, offsets, x2, final_norm[None, :], ys)
    return out.reshape(bsz, seq, d)
```

```python
import functools
import math

import jax
import jax.numpy as jnp
from jax import lax
from jax.experimental import pallas as pl
from jax.experimental.pallas import tpu as pltpu

D_MODEL = 1024
ROPE_THETA = 10000.0
NORM_EPS = 1e-6

HEAD_DIM = 64
MOBA_HEADS = 8
MOBA_BLOCK = 256
MOBA_TOPK = 3
DIFF_HEADS = 4
EVEN_IN = 3072

MLA_HEADS = 16
MLA_Q_RANK = 256
MLA_KV_RANK = 128
MLA_NOPE = 64
MLA_ROPE = 32
MLA_V = 64

N_GROUPS = 4
EXPERTS_PER_GROUP = 8
N_EXPERTS = 32
EXPERT_FF = 256
PAIRS_PER_GROUP = EXPERTS_PER_GROUP * (EXPERTS_PER_GROUP - 1) // 2
N_CLASSES = N_GROUPS * PAIRS_PER_GROUP

LANES = 128
LOG2E = math.log2(math.e)
MASK_BIAS = -1e30
VMEM_LIMIT = 48 * 1024 * 1024

TOKEN_TILE = 512
EXPERT_TILE = 256
ROW_W = D_MODEL + 128
SUBLANES = 8
Q_TILE = 512
Q_GROUP = 4
KV_TILE = 1024
SUB_TILE = 256

_F32 = jnp.float32
_BF16 = jnp.bfloat16


def _dot(a, b):
    return jnp.dot(a, b, preferred_element_type=_F32)


def _rms(x, g):
    return x * lax.rsqrt(jnp.mean(x * x, axis=-1, keepdims=True) + NORM_EPS) * g


def _params(*sem):
    return pltpu.CompilerParams(dimension_semantics=sem, vmem_limit_bytes=VMEM_LIMIT)


def _rope_tables(seq, dim):
    inv = 1.0 / (ROPE_THETA ** (jnp.arange(0, dim, 2, dtype=_F32) / dim))
    ang = jnp.arange(seq, dtype=_F32)[:, None] * inv[None, :]
    return jnp.cos(ang), jnp.sin(ang)


def _even_rope_tables(seq):
    cos, sin = _rope_tables(seq, HEAD_DIM)
    half = HEAD_DIM // 2
    lane = jnp.arange(LANES)
    first = (lane % HEAD_DIM) < half
    c = cos[:, lane % half]
    s = sin[:, lane % half]
    return c, jnp.where(first, -s, 0.0), jnp.where(first, 0.0, s)


def _mla_rope_tables(seq, scale):
    cos, sin = _rope_tables(seq, MLA_ROPE)
    half = MLA_ROPE // 2
    lane = jnp.arange(LANES)
    r = (lane - MLA_NOPE) % half
    in_rope = (lane >= MLA_NOPE) & (lane < MLA_NOPE + MLA_ROPE)
    first = (lane >= MLA_NOPE) & (lane < MLA_NOPE + half)
    second = (lane >= MLA_NOPE + half) & (lane < MLA_NOPE + MLA_ROPE)
    c, s = cos[:, r], sin[:, r]
    cq = jnp.where(lane < MLA_NOPE, 1.0, jnp.where(in_rope, c, 0.0)) * scale
    ck = jnp.where(in_rope, c, 0.0)
    sa = jnp.where(first, -s, 0.0)
    sb = jnp.where(second, s, 0.0)
    return cq, sa * scale, sb * scale, ck, sa, sb


def _rotate(z, c, sa, sb, half):
    return z * c + pltpu.roll(z, LANES - half, 1) * sa + pltpu.roll(z, half, 1) * sb


def _store_transposed(vt_ref, row0, z):
    vt_ref[row0:row0 + LANES, :] = z.T.astype(_BF16)


def _vt_spec(features, tm):
    per_slab = KV_TILE // tm
    return pl.BlockSpec((None, features, tm), lambda i: (i // per_slab, 0, i % per_slab))


def _even_proj_kernel(x_ref, g_ref, w_ref, c_ref, sa_ref, sb_ref, qk_ref, vt_ref):
    h = _rms(x_ref[...], g_ref[...]).astype(_BF16)
    c, sa, sb = c_ref[...], sa_ref[...], sb_ref[...]
    chunk = 512
    blocks = chunk // LANES
    qk_chunk = {0: 0, 1: 1, 3: 2, 4: 3}
    v_chunk = {2: 0, 5: 1}
    for ci in range(EVEN_IN // chunk):
        z = _dot(h, w_ref[:, ci * chunk:(ci + 1) * chunk])
        for b in range(blocks):
            zz = z[:, b * LANES:(b + 1) * LANES]
            if ci in v_chunk:
                _store_transposed(vt_ref, (v_chunk[ci] * blocks + b) * LANES, zz)
                continue
            zz = _rotate(zz, c, sa, sb, HEAD_DIM // 2)
            if ci in (0, 3):
                zz = zz * (HEAD_DIM ** -0.5 * LOG2E)
            col = (qk_chunk[ci] * blocks + b) * LANES
            qk_ref[:, col:col + LANES] = zz.astype(_BF16)


def _even_proj(x2, g, w, tabs, seq):
    n = x2.shape[0]
    tm = TOKEN_TILE
    spb = seq // tm
    tab_spec = pl.BlockSpec((tm, LANES), lambda i: (i % spb, 0))
    qk_w = 4 * 512
    v_w = 2 * 512
    return pl.pallas_call(
        _even_proj_kernel,
        grid=(n // tm,),
        in_specs=[
            pl.BlockSpec((tm, D_MODEL), lambda i: (i, 0)),
            pl.BlockSpec((1, D_MODEL), lambda i: (0, 0)),
            pl.BlockSpec((D_MODEL, EVEN_IN), lambda i: (0, 0)),
            tab_spec, tab_spec, tab_spec,
        ],
        out_specs=[pl.BlockSpec((tm, qk_w), lambda i: (i, 0)), _vt_spec(v_w, tm)],
        out_shape=[jax.ShapeDtypeStruct((n, qk_w), _BF16),
                   jax.ShapeDtypeStruct((n // KV_TILE, v_w, KV_TILE), _BF16)],
        compiler_params=_params("parallel"),
        name="even_proj",
    )(x2, g, w, *tabs)


class _AttnRefs:
    def __init__(self, s_ref, pm_ref, sd_ref, pmd_ref, m_ref, l_ref, acc_ref):
        self.main = [(s_ref, pm_ref, 0), (s_ref, pm_ref, 1)]
        self.diag = [(sd_ref, pmd_ref, 0), (sd_ref, pmd_ref, 1)]
        self.m, self.l, self.acc = m_ref, l_ref, acc_ref


def _attn_scratch(dv, tq):
    scores = pltpu.VMEM((2, 2, KV_TILE, tq), _F32)
    maxima = pltpu.VMEM((2, 2, 8, tq), _F32)
    return [scores, maxima, scores, maxima,
            pltpu.VMEM((2, 1, tq), _F32),
            pltpu.VMEM((2, 1, tq), _F32),
            pltpu.VMEM((2, dv, tq), _F32)]


def _kv_rows(c):
    return pl.ds(pl.multiple_of(c * KV_TILE, KV_TILE), KV_TILE)


def _sub_rows(c, r):
    return pl.ds(pl.multiple_of(c * KV_TILE + r * SUB_TILE, SUB_TILE), SUB_TILE)


def _fold8(x, op):
    return op(x.reshape(x.shape[0] // 8, 8, x.shape[1]), axis=0)


def _stage(refs, value, softmax_of=None, scores_of=None):
    chains = range(2)
    if softmax_of is not None:
        (s_in, pm_in, b_in), c = softmax_of
        m_new, alpha, acc, lsum = [], [], [], [None, None]
        for t in chains:
            m_prev = refs.m[t]
            m_new.append(jnp.maximum(m_prev, jnp.max(pm_in[b_in, t], axis=0, keepdims=True)))
            alpha.append(jnp.exp2(m_prev - m_new[t]))
            acc.append(alpha[t] * refs.acc[t])
    if scores_of is not None:
        (s_out, pm_out, b_out), c_out, score, mask = scores_of
        pm = [None, None]
    for r in range(KV_TILE // SUB_TILE):
        rows = slice(r * SUB_TILE, (r + 1) * SUB_TILE)
        for t in chains:
            if scores_of is not None:
                s = score(t, c_out, r)
                if mask is not None:
                    s = mask(s, c_out, r)
                s_out[b_out, t, rows, :] = s
                part = _fold8(s, jnp.max)
                pm[t] = part if pm[t] is None else jnp.maximum(pm[t], part)
            if softmax_of is not None:
                p = jnp.exp2(s_in[b_in, t, rows, :] - m_new[t])
                part = _fold8(p, jnp.sum)
                lsum[t] = part if lsum[t] is None else lsum[t] + part
                acc[t] = acc[t] + _dot(value(t, c, r), p.astype(_BF16))
    for t in chains:
        if scores_of is not None:
            pm_out[b_out, t] = pm[t]
        if softmax_of is not None:
            refs.m[t] = m_new[t]
            refs.l[t] = alpha[t] * refs.l[t] + jnp.sum(lsum[t], axis=0, keepdims=True)
            refs.acc[t] = acc[t]


def _sweep(refs, first_qi, make_score, value, finalize):
    tq = refs.m.shape[-1]
    m0, m1 = refs.main

    def plan(j):
        qi = first_qi + j
        last = (qi * tq) // KV_TILE

        def causal(s, c, r):
            key = lax.broadcasted_iota(jnp.int32, s.shape, 0) + (c * KV_TILE + r * SUB_TILE)
            qry = lax.broadcasted_iota(jnp.int32, s.shape, 1) + qi * tq
            return jnp.where(key <= qry, s, -jnp.inf)

        return last, (refs.diag[j % 2], last, make_score(j), causal)

    plans = [plan(j) for j in range(Q_GROUP)]
    _stage(refs, value, scores_of=plans[0][1])

    for j, (last, (diag, _, score, _)) in enumerate(plans):
        refs.m[...] = jnp.full(refs.m.shape, -jnp.inf, _F32)
        refs.l[...] = jnp.zeros(refs.l.shape, _F32)
        refs.acc[...] = jnp.zeros(refs.acc.shape, _F32)
        odd = last % 2

        @pl.when(odd == 1)
        def _():
            _stage(refs, value, softmax_of=(diag, last), scores_of=(m1, 0, score, None))

        @pl.when(jnp.logical_and(odd == 0, last >= 2))
        def _():
            _stage(refs, value, softmax_of=(diag, last), scores_of=(m0, 0, score, None))
            _stage(refs, value, softmax_of=(m0, 0), scores_of=(m1, 1, score, None))

        base = 1 - odd

        def pair(k, carry):
            c = base + 2 * k
            _stage(refs, value, softmax_of=(m1, c), scores_of=(m0, c + 1, score, None))
            _stage(refs, value, softmax_of=(m0, c + 1), scores_of=(m1, c + 2, score, None))
            return carry

        lax.fori_loop(0, lax.shift_right_arithmetic(last - 1 - base, 1), pair, 0)

        nxt = plans[j + 1][1] if j + 1 < Q_GROUP else None

        @pl.when(last == 0)
        def _():
            _stage(refs, value, softmax_of=(diag, last), scores_of=nxt)

        @pl.when(last > 0)
        def _():
            _stage(refs, value, softmax_of=(m1, last - 1), scores_of=nxt)

        finalize(j)


def _transposed(q):
    return q.astype(_F32).T


def _moba_kernel(q_ref, k_ref, vt_ref, o_ref, km_ref, ka_ref, *scratch, nb, topk):
    step = pl.program_id(2)
    tq = Q_TILE
    half = HEAD_DIM // 2

    @pl.when(step == 0)
    def _():
        kf = k_ref[...].astype(_F32)
        means = jnp.sum(kf.reshape(nb, MOBA_BLOCK, LANES), axis=1) * (1.0 / MOBA_BLOCK)
        lane_b = lax.broadcasted_iota(jnp.int32, (nb, LANES), 1)
        km_ref[...] = jnp.zeros(km_ref.shape, _F32)
        km_ref[0:nb, :] = jnp.where(lane_b < HEAD_DIM, 0.0, means)
        km_ref[HEAD_DIM:HEAD_DIM + nb, :] = jnp.where(lane_b < HEAD_DIM, means, 0.0)

        lane = lax.broadcasted_iota(jnp.int32, (KV_TILE, LANES), 1)
        blk = lax.broadcasted_iota(jnp.int32, (KV_TILE, LANES), 0) // MOBA_BLOCK

        def build(c, carry):
            rows = _kv_rows(c)
            k = k_ref[rows, :].astype(_F32)
            b = blk + c * (KV_TILE // MOBA_BLOCK)
            ka_ref[0, rows, :] = jnp.where(lane < HEAD_DIM, k,
                                           jnp.where(lane == b + HEAD_DIM, 1.0, 0.0)).astype(_BF16)
            ka_ref[1, rows, :] = jnp.where(lane < HEAD_DIM,
                                           jnp.where(lane == b, 1.0, 0.0), k).astype(_BF16)
            return carry

        lax.fori_loop(0, k_ref.shape[0] // KV_TILE, build, 0)

    row = lax.broadcasted_iota(jnp.int32, (half, tq), 0).astype(_F32)
    z_half = jnp.zeros((half, tq), _F32)

    def make_score(j):
        qt = _transposed(q_ref[j * tq:(j + 1) * tq, :])
        gate = _dot(km_ref[...].astype(_BF16), qt.astype(_BF16))
        qpos = (step * Q_GROUP + j) * tq + lax.broadcasted_iota(jnp.int32, (half, tq), 1)
        own = (qpos // MOBA_BLOCK).astype(_F32)

        def block_bias(base):
            g = jnp.where(row < own, gate[base:base + half], -jnp.inf)
            bias = jnp.where(row == own, 0.0, MASK_BIAS)
            for _ in range(topk):
                mx = jnp.max(g, axis=0, keepdims=True)
                hit = (g == mx) & (mx > -jnp.inf)
                idx = jnp.min(jnp.where(hit, row, 4.0 * LANES), axis=0, keepdims=True)
                pick = row == idx
                bias = jnp.where(pick, 0.0, bias)
                g = jnp.where(pick, -jnp.inf, g)
            return bias

        qs = (jnp.concatenate([qt[:HEAD_DIM], block_bias(HEAD_DIM), z_half], axis=0).astype(_BF16),
              jnp.concatenate([block_bias(0), z_half, qt[HEAD_DIM:]], axis=0).astype(_BF16))
        return lambda t, c, r: _dot(ka_ref[t, _sub_rows(c, r), :], qs[t])

    def value(t, c, r):
        return vt_ref[c, t * HEAD_DIM:(t + 1) * HEAD_DIM, r * SUB_TILE:(r + 1) * SUB_TILE]

    refs = _AttnRefs(*scratch)

    def finalize(j):
        ot = jnp.concatenate([refs.acc[0] / refs.l[0], refs.acc[1] / refs.l[1]], axis=0)
        o_ref[j * tq:(j + 1) * tq, :] = ot.T.astype(o_ref.dtype)

    _sweep(refs, step * Q_GROUP, make_score, value, finalize)


def _moba(qk3, vt):
    bsz, seq, _ = qk3.shape
    tq = Q_TILE * Q_GROUP
    assert Q_TILE % MOBA_BLOCK == 0 and seq % KV_TILE == 0 and seq % tq == 0
    nb = seq // MOBA_BLOCK
    assert nb <= HEAD_DIM // 2
    topk = max(1, min(MOBA_TOPK, nb - 1))
    pairs = MOBA_HEADS // 2
    return pl.pallas_call(
        functools.partial(_moba_kernel, nb=nb, topk=topk),
        grid=(bsz, pairs, seq // tq),
        in_specs=[
            pl.BlockSpec((None, tq, LANES), lambda b, p, i: (b, i, p)),
            pl.BlockSpec((None, seq, LANES), lambda b, p, i: (b, 0, pairs + p)),
            pl.BlockSpec((seq // KV_TILE, LANES, KV_TILE), lambda b, p, i: (b, p, 0)),
        ],
        out_specs=pl.BlockSpec((None, tq, LANES), lambda b, p, i: (b, i, p)),
        out_shape=jax.ShapeDtypeStruct((bsz, seq, pairs * LANES), _BF16),
        scratch_shapes=[pltpu.VMEM((LANES, LANES), _F32), pltpu.VMEM((2, seq, LANES), _BF16)]
                       + _attn_scratch(HEAD_DIM, Q_TILE),
        compiler_params=_params("parallel", "parallel", "arbitrary"),
        name="moba_attn",
    )(qk3, qk3, vt)


def _diff_kernel(q_ref, k_ref, vt_ref, lam_ref, g_ref, o_ref, *scratch, lam_init):
    step = pl.program_id(2)
    tq = Q_TILE
    z_head = jnp.zeros((HEAD_DIM, tq), _F32)

    def make_score(j):
        qt = _transposed(q_ref[j * tq:(j + 1) * tq, :])
        qd = (jnp.concatenate([qt[:HEAD_DIM], z_head], axis=0).astype(_BF16),
              jnp.concatenate([z_head, qt[HEAD_DIM:]], axis=0).astype(_BF16))
        return lambda t, c, r: _dot(k_ref[_sub_rows(c, r), :], qd[t])

    def value(t, c, r):
        return vt_ref[c, :, r * SUB_TILE:(r + 1) * SUB_TILE]

    lv = lam_ref[...]
    lam = (jnp.exp(jnp.sum(lv[0:1] * lv[1:2], axis=1, keepdims=True))
           - jnp.exp(jnp.sum(lv[2:3] * lv[3:4], axis=1, keepdims=True)) + lam_init)
    refs = _AttnRefs(*scratch)

    def finalize(j):
        ot = refs.acc[0] / refs.l[0] - lam * (refs.acc[1] / refs.l[1])
        o_ref[j * tq:(j + 1) * tq, :] = (_rms(ot.T, g_ref[...]) * (1.0 - lam_init)).astype(o_ref.dtype)

    _sweep(refs, step * Q_GROUP, make_score, value, finalize)


def _diff(qk3, vt, lam_rows, subln, lam_init):
    bsz, seq, _ = qk3.shape
    tq = Q_TILE * Q_GROUP
    base = 2 * MOBA_HEADS * HEAD_DIM // LANES
    vbase = MOBA_HEADS * HEAD_DIM // LANES
    return pl.pallas_call(
        functools.partial(_diff_kernel, lam_init=lam_init),
        grid=(bsz, DIFF_HEADS, seq // tq),
        in_specs=[
            pl.BlockSpec((None, tq, LANES), lambda b, h, i: (b, i, base + h)),
            pl.BlockSpec((None, seq, LANES), lambda b, h, i: (b, 0, base + DIFF_HEADS + h)),
            pl.BlockSpec((seq // KV_TILE, LANES, KV_TILE), lambda b, h, i: (b, vbase + h, 0)),
            pl.BlockSpec((4, HEAD_DIM), lambda b, h, i: (0, 0)),
            pl.BlockSpec((1, LANES), lambda b, h, i: (0, 0)),
        ],
        out_specs=pl.BlockSpec((None, tq, LANES), lambda b, h, i: (b, i, h)),
        out_shape=jax.ShapeDtypeStruct((bsz, seq, DIFF_HEADS * LANES), _BF16),
        scratch_shapes=_attn_scratch(LANES, Q_TILE),
        compiler_params=_params("parallel", "parallel", "arbitrary"),
        name="diff_attn",
    )(qk3, qk3, vt, lam_rows, subln)


def _mla_proj_kernel(dest_ref, x_ref, ys_ref,
                     g_ref, w1_ref, qg_ref, kg_ref, wq_ref, wk_ref, wv_ref,
                     cq_ref, saq_ref, sbq_ref, ck_ref, sak_ref, sbk_ref,
                     x_out, q_out, k_out, vt_out, buf_ref, sem):
    x = x_ref[...] + _gather_rows(dest_ref, ys_ref, buf_ref, sem)
    x_out[...] = x
    h = _rms(x, g_ref[...]).astype(_BF16)
    z = _dot(h, w1_ref[...])
    cq = _rms(z[:, :MLA_Q_RANK], qg_ref[...]).astype(_BF16)
    ckv = _rms(z[:, MLA_Q_RANK:MLA_Q_RANK + MLA_KV_RANK], kg_ref[...]).astype(_BF16)
    kr = z[:, MLA_Q_RANK + MLA_KV_RANK:]
    half = MLA_ROPE // 2
    kr = _rotate(kr, ck_ref[...], sak_ref[...], sbk_ref[...], half)
    qf = _dot(cq, wq_ref[...])
    kf = _dot(ckv, wk_ref[...])
    cqt, saq, sbq = cq_ref[...], saq_ref[...], sbq_ref[...]
    for hd in range(MLA_HEADS):
        sl = slice(hd * LANES, (hd + 1) * LANES)
        q_out[:, sl] = _rotate(qf[:, sl], cqt, saq, sbq, half).astype(_BF16)
        k_out[:, sl] = (kf[:, sl] + kr).astype(_BF16)
    vf = _dot(ckv, wv_ref[...])
    for b in range(MLA_HEADS * MLA_V // LANES):
        _store_transposed(vt_out, b * LANES, vf[:, b * LANES:(b + 1) * LANES])


def _mla_proj(dest, x2, ys, g, w1, qg, kg, wq, wk, wv, tabs, seq):
    n = x2.shape[0]
    tm = TOKEN_TILE
    spb = seq // tm
    hw = MLA_HEADS * LANES
    vw = MLA_HEADS * MLA_V
    tab_spec = pl.BlockSpec((tm, LANES), lambda i: (i % spb, 0))

    def full(a):
        return pl.BlockSpec(a.shape, lambda i: (0,) * a.ndim)

    return pl.pallas_call(
        _mla_proj_kernel,
        grid=(n // tm,),
        in_specs=[_index_spec(tm),
                  pl.BlockSpec((tm, D_MODEL), lambda i: (i, 0)),
                  pl.BlockSpec(memory_space=pl.ANY),
                  full(g), full(w1), full(qg), full(kg), full(wq), full(wk), full(wv)]
                 + [tab_spec] * 6,
        out_specs=[pl.BlockSpec((tm, D_MODEL), lambda i: (i, 0)),
                   pl.BlockSpec((tm, hw), lambda i: (i, 0)),
                   pl.BlockSpec((tm, hw), lambda i: (i, 0)),
                   _vt_spec(vw, tm)],
        out_shape=[jax.ShapeDtypeStruct((n, D_MODEL), _F32),
                   jax.ShapeDtypeStruct((n, hw), _BF16),
                   jax.ShapeDtypeStruct((n, hw), _BF16),
                   jax.ShapeDtypeStruct((n // KV_TILE, vw, KV_TILE), _BF16)],
        scratch_shapes=_gather_scratch(tm),
        compiler_params=_params("arbitrary"),
        name="mla_proj",
    )(dest, x2, ys, g, w1, qg, kg, wq, wk, wv, *tabs)


def _mla_kernel(q_ref, k_ref, vt_ref, o_ref, *scratch):
    step = pl.program_id(2)
    tq = Q_TILE

    def make_score(j):
        q = q_ref[j * tq:(j + 1) * tq, :]
        qd = tuple(_transposed(q[:, t * LANES:(t + 1) * LANES]).astype(_BF16) for t in range(2))
        return lambda t, c, r: _dot(k_ref[_sub_rows(c, r), t * LANES:(t + 1) * LANES], qd[t])

    def value(t, c, r):
        return vt_ref[c, t * MLA_V:(t + 1) * MLA_V, r * SUB_TILE:(r + 1) * SUB_TILE]

    refs = _AttnRefs(*scratch)

    def finalize(j):
        ot = jnp.concatenate([refs.acc[0] / refs.l[0], refs.acc[1] / refs.l[1]], axis=0)
        o_ref[j * tq:(j + 1) * tq, :] = ot.T.astype(o_ref.dtype)

    _sweep(refs, step * Q_GROUP, make_score, value, finalize)


def _mla(q3, k3, vt):
    bsz, seq, _ = q3.shape
    tq = Q_TILE * Q_GROUP
    pairs = MLA_HEADS // 2
    return pl.pallas_call(
        _mla_kernel,
        grid=(bsz, pairs, seq // tq),
        in_specs=[
            pl.BlockSpec((None, tq, 2 * LANES), lambda b, p, i: (b, i, p)),
            pl.BlockSpec((None, seq, 2 * LANES), lambda b, p, i: (b, 0, p)),
            pl.BlockSpec((seq // KV_TILE, LANES, KV_TILE), lambda b, p, i: (b, p, 0)),
        ],
        out_specs=pl.BlockSpec((None, tq, LANES), lambda b, p, i: (b, i, p)),
        out_shape=jax.ShapeDtypeStruct((bsz, seq, pairs * LANES), _BF16),
        scratch_shapes=_attn_scratch(MLA_V, Q_TILE),
        compiler_params=_params("parallel", "parallel", "arbitrary"),
        name="mla_attn",
    )(q3, k3, vt)


def _route(logits):
    lane = lax.broadcasted_iota(jnp.int32, logits.shape, 1).astype(_F32)
    big = 4.0 * LANES
    is_group = (lane >= N_EXPERTS) & (lane < N_EXPERTS + N_GROUPS)
    gl = jnp.where(is_group, logits, -jnp.inf)
    gmax = jnp.max(gl, axis=1, keepdims=True)
    grp = jnp.min(jnp.where(gl == gmax, lane, big), axis=1, keepdims=True) - N_EXPERTS
    p_group = 1.0 / jnp.sum(jnp.exp(gl - gmax), axis=1, keepdims=True)
    in_group = (lane >= grp * EXPERTS_PER_GROUP) & (lane < (grp + 1) * EXPERTS_PER_GROUP)
    el = jnp.where(in_group, logits, -jnp.inf)
    e1 = jnp.max(el, axis=1, keepdims=True)
    i1 = jnp.min(jnp.where(el == e1, lane, big), axis=1, keepdims=True)
    el2 = jnp.where(lane == i1, -jnp.inf, el)
    e2 = jnp.max(el2, axis=1, keepdims=True)
    i2 = jnp.min(jnp.where(el2 == e2, lane, big), axis=1, keepdims=True)
    r = jnp.exp(e2 - e1)
    w1 = p_group / (1.0 + r)
    w2 = w1 * r
    first = i1 < i2
    a = jnp.minimum(i1, i2) - grp * EXPERTS_PER_GROUP
    b = jnp.maximum(i1, i2) - grp * EXPERTS_PER_GROUP
    cls = grp * PAIRS_PER_GROUP + a * (2 * EXPERTS_PER_GROUP - 1 - a) * 0.5 + (b - a - 1.0)
    return jnp.where(first, w1, w2), jnp.where(first, w2, w1), cls


def _lane_dense(col, eye):
    t = col.shape[0]
    return jnp.sum(jnp.where(eye, col, 0.0).reshape(t // LANES, LANES, LANES), axis=1)


def _out_proj_kernel(*refs, n_mix):
    x_ref = refs[0]
    mix = refs[1:1 + 2 * n_mix]
    g_ref, wr_ref, br_ref = refs[1 + 2 * n_mix:4 + 2 * n_mix]
    x_out, row_out, cls_out, rank_out, count_out, count_ref = refs[4 + 2 * n_mix:]

    @pl.when(pl.program_id(0) == 0)
    def _():
        count_ref[...] = jnp.zeros(count_ref.shape, _F32)

    x = x_ref[...]
    for i in range(n_mix):
        x = x + _dot(mix[2 * i][...], mix[2 * i + 1][...])
    x_out[...] = x
    t = _rms(x, g_ref[...])
    w_lo, w_hi, cls = _route(_dot(t.astype(_BF16), wr_ref[...]) + br_ref[...])

    tm = x.shape[0]
    lane = lax.broadcasted_iota(jnp.int32, (tm, LANES), 1)
    row_out[:, :D_MODEL] = t
    row_out[:, D_MODEL:] = jnp.where(lane == 0, w_lo, 0.0) + jnp.where(lane == 1, w_hi, 0.0)

    onehot = jnp.where(lane.astype(_F32) == cls, 1.0, 0.0)
    earlier = (lax.broadcasted_iota(jnp.int32, (tm, tm), 1)
               < lax.broadcasted_iota(jnp.int32, (tm, tm), 0))
    before = _dot(jnp.where(earlier, 1.0, 0.0).astype(_BF16), onehot.astype(_BF16))
    rank = jnp.sum(onehot * (before + count_ref[...]), axis=1, keepdims=True)
    count_ref[...] += jnp.sum(onehot, axis=0, keepdims=True)
    count_out[...] = count_ref[...].astype(jnp.int32)
    eye = lax.broadcasted_iota(jnp.int32, (tm, LANES), 0) % LANES == lane
    cls_out[...] = _lane_dense(cls, eye).astype(jnp.int32)
    rank_out[...] = _lane_dense(rank, eye).astype(jnp.int32)


def _out_proj(x2, mixes, g, wr, br):
    n = x2.shape[0]
    tm = TOKEN_TILE
    in_specs = [pl.BlockSpec((tm, D_MODEL), lambda i: (i, 0))]
    args = [x2]
    for o, w in mixes:
        in_specs.append(pl.BlockSpec((tm, o.shape[1]), lambda i: (i, 0)))
        in_specs.append(pl.BlockSpec(w.shape, lambda i: (0, 0)))
        args += [o, w]
    for a in (g, wr, br):
        in_specs.append(pl.BlockSpec(a.shape, lambda i: (0, 0)))
        args.append(a)
    idx_spec = pl.BlockSpec((None, tm // LANES, LANES), lambda i: (i, 0, 0))
    idx_shape = jax.ShapeDtypeStruct((n // tm, tm // LANES, LANES), jnp.int32)
    return pl.pallas_call(
        functools.partial(_out_proj_kernel, n_mix=len(mixes)),
        grid=(n // tm,),
        in_specs=in_specs,
        out_specs=[pl.BlockSpec((tm, D_MODEL), lambda i: (i, 0)),
                   pl.BlockSpec((tm, ROW_W), lambda i: (i, 0)),
                   idx_spec, idx_spec,
                   pl.BlockSpec((1, LANES), lambda i: (0, 0))],
        out_shape=[jax.ShapeDtypeStruct((n, D_MODEL), _F32),
                   jax.ShapeDtypeStruct((n, ROW_W), _F32),
                   idx_shape, idx_shape,
                   jax.ShapeDtypeStruct((1, LANES), jnp.int32)],
        scratch_shapes=[pltpu.VMEM((1, LANES), _F32)],
        compiler_params=_params("arbitrary"),
        name="out_proj_route",
    )(*args)


def _class_experts():
    pairs = [(a, b) for a in range(EXPERTS_PER_GROUP) for b in range(a + 1, EXPERTS_PER_GROUP)]
    lo = [g * EXPERTS_PER_GROUP + a for g in range(N_GROUPS) for a, _ in pairs]
    hi = [g * EXPERTS_PER_GROUP + b for g in range(N_GROUPS) for _, b in pairs]
    return jnp.asarray(lo, jnp.int32), jnp.asarray(hi, jnp.int32)


def _tile_plan(counts, max_tiles):
    tiles = (counts + EXPERT_TILE - 1) // EXPERT_TILE
    ends = jnp.cumsum(tiles)
    offsets = (ends - tiles) * EXPERT_TILE
    tile = jnp.arange(max_tiles, dtype=jnp.int32)
    tile_cls = jnp.sum((ends[None, :] <= tile[:, None]).astype(jnp.int32), axis=1)
    tile_cls = jnp.minimum(tile_cls, N_CLASSES - 1)
    lo, hi = _class_experts()
    return offsets.astype(jnp.int32), lo[tile_cls], hi[tile_cls], ends[-1:].astype(jnp.int32)


def _sorted_rows_kernel(cls_ref, rank_ref, off_ref, dest_ref):
    table = jnp.broadcast_to(off_ref[...], cls_ref.shape)
    dest_ref[...] = jnp.take_along_axis(table, cls_ref[...], axis=1) + rank_ref[...]


def _sorted_rows(cls, rank, offsets):
    spec = pl.BlockSpec((None,) + cls.shape[1:], lambda i: (i, 0, 0))
    return pl.pallas_call(
        _sorted_rows_kernel,
        grid=(cls.shape[0],),
        in_specs=[spec, spec, pl.BlockSpec((1, LANES), lambda i: (0, 0))],
        out_specs=spec,
        out_shape=jax.ShapeDtypeStruct(cls.shape, jnp.int32),
        compiler_params=_params("parallel"),
        name="moe_sorted_rows",
    )(cls, rank, offsets[None, :])


def _row_copies(dest_ref, copy):
    tiles_per_row = LANES // SUBLANES

    def each(method):
        for a in range(dest_ref.shape[0]):
            def body(g, carry, a=a):
                for j in range(SUBLANES):
                    d = dest_ref[a, g * SUBLANES + j]
                    getattr(copy(a * tiles_per_row + g, j, d), method)()
                return carry

            lax.fori_loop(0, tiles_per_row, body, 0)

    each("start")
    each("wait")


def _dispatch_kernel(dest_ref, row_ref, xs_in_ref, xs_ref, sem):
    del xs_in_ref

    def copy(tile, j, d):
        return pltpu.make_async_copy(row_ref.at[tile, pl.ds(j, 1), :],
                                     xs_ref.at[pl.ds(d, 1), :], sem)

    _row_copies(dest_ref, copy)


def _index_spec(tm):
    return pl.BlockSpec((None, tm // LANES, LANES), lambda i: (i, 0, 0), memory_space=pltpu.SMEM)


def _dispatch(dest, rows, max_tiles):
    n = rows.shape[0]
    tm = TOKEN_TILE
    xs0 = jnp.zeros((max_tiles * EXPERT_TILE, ROW_W), _F32)
    return pl.pallas_call(
        _dispatch_kernel,
        grid=(n // tm,),
        in_specs=[_index_spec(tm),
                  pl.BlockSpec((tm // SUBLANES, SUBLANES, ROW_W), lambda i: (i, 0, 0)),
                  pl.BlockSpec(memory_space=pl.ANY)],
        out_specs=pl.BlockSpec(memory_space=pl.ANY),
        out_shape=jax.ShapeDtypeStruct(xs0.shape, _F32),
        scratch_shapes=[pltpu.SemaphoreType.DMA(())],
        input_output_aliases={2: 0},
        compiler_params=_params("arbitrary"),
        name="moe_dispatch",
    )(dest, rows.reshape(n // SUBLANES, SUBLANES, ROW_W), xs0)


def _expert_kernel(lo_ref, hi_ref, used_ref, xs_ref, wg0, wu0, wd0, wg1, wu1, wd1, ys_ref):
    del lo_ref, hi_ref
    i = pl.program_id(0)

    @pl.when(i < used_ref[0])
    def _():
        xs = xs_ref[...]
        x = xs[:, :D_MODEL].astype(_BF16)
        gates = xs[:, D_MODEL:]
        y = None
        for k, (wg, wu, wd) in enumerate(((wg0, wu0, wd0), (wg1, wu1, wd1))):
            hg = _dot(x, wg[...])
            hu = _dot(x, wu[...])
            act = hg * (1.0 / (1.0 + jnp.exp(-hg))) * hu * gates[:, k:k + 1]
            yk = _dot(act.astype(_BF16), wd[...])
            y = yk if y is None else y + yk
        ys_ref[...] = y

    @pl.when(i >= used_ref[0])
    def _():
        ys_ref[...] = jnp.zeros(ys_ref.shape, _F32)


def _experts(xs, tile_lo, tile_hi, used, wg, wu, wd):
    max_tiles = xs.shape[0] // EXPERT_TILE
    up = (None, D_MODEL, EXPERT_FF)
    down = (None, EXPERT_FF, D_MODEL)
    by_lo = lambda i, lo, hi, used: (lo[i], 0, 0)
    by_hi = lambda i, lo, hi, used: (hi[i], 0, 0)
    grid_spec = pltpu.PrefetchScalarGridSpec(
        num_scalar_prefetch=3,
        grid=(max_tiles,),
        in_specs=[pl.BlockSpec((EXPERT_TILE, ROW_W), lambda i, lo, hi, used: (i, 0)),
                  pl.BlockSpec(up, by_lo), pl.BlockSpec(up, by_lo), pl.BlockSpec(down, by_lo),
                  pl.BlockSpec(up, by_hi), pl.BlockSpec(up, by_hi), pl.BlockSpec(down, by_hi)],
        out_specs=pl.BlockSpec((EXPERT_TILE, D_MODEL), lambda i, lo, hi, used: (i, 0)),
    )
    return pl.pallas_call(
        _expert_kernel,
        grid_spec=grid_spec,
        out_shape=jax.ShapeDtypeStruct((xs.shape[0], D_MODEL), _F32),
        compiler_params=_params("arbitrary"),
        name="moe_experts",
    )(tile_lo, tile_hi, used, xs, wg, wu, wd, wg, wu, wd)


def _gather_rows(dest_ref, ys_ref, buf_ref, sem):
    def copy(tile, j, d):
        return pltpu.make_async_copy(ys_ref.at[pl.ds(d, 1), :],
                                     buf_ref.at[tile, pl.ds(j, 1), :], sem)

    _row_copies(dest_ref, copy)
    return buf_ref[...].reshape(buf_ref.shape[0] * SUBLANES, buf_ref.shape[2])


def _gather_scratch(tm):
    return [pltpu.VMEM((tm // SUBLANES, SUBLANES, D_MODEL), _F32), pltpu.SemaphoreType.DMA(())]


def _final_kernel(dest_ref, x_ref, g_ref, ys_ref, o_ref, buf_ref, sem):
    o_ref[...] = _rms(x_ref[...] + _gather_rows(dest_ref, ys_ref, buf_ref, sem), g_ref[...])


def _final(dest, x2, g, ys):
    n = x2.shape[0]
    tm = TOKEN_TILE
    return pl.pallas_call(
        _final_kernel,
        grid=(n // tm,),
        in_specs=[_index_spec(tm),
                  pl.BlockSpec((tm, D_MODEL), lambda i: (i, 0)),
                  pl.BlockSpec((1, D_MODEL), lambda i: (0, 0)),
                  pl.BlockSpec(memory_space=pl.ANY)],
        out_specs=pl.BlockSpec((tm, D_MODEL), lambda i: (i, 0)),
        out_shape=jax.ShapeDtypeStruct((n, D_MODEL), _F32),
        scratch_shapes=_gather_scratch(tm),
        compiler_params=_params("arbitrary"),
        name="moe_combine_norm",
    )(dest, x2, g, ys)


def _moe(rows, cls, rank, counts, wg, wu, wd):
    n = rows.shape[0]
    max_tiles = n // EXPERT_TILE + N_CLASSES
    offsets, tile_lo, tile_hi, used = _tile_plan(counts[0], max_tiles)
    dest = _sorted_rows(cls, rank, offsets)
    xs = _dispatch(dest, rows, max_tiles)
    return _experts(xs, tile_lo, tile_hi, used, wg, wu, wd), dest


def _router_weights(w_group, b_group, w_expert, b_expert):
    pad = LANES - N_EXPERTS - N_GROUPS
    wr = jnp.concatenate([w_expert, w_group, jnp.zeros((D_MODEL, pad), _F32)], axis=1)
    br = jnp.concatenate([b_expert, b_group, jnp.zeros((pad,), _F32)])[None, :]
    return wr.astype(_BF16), br


def _mla_weights(w_in, w_q_up, w_kv_up):
    lat = MLA_Q_RANK + MLA_KV_RANK
    w1 = jnp.concatenate([w_in[:, :lat], jnp.zeros((D_MODEL, MLA_NOPE), _F32), w_in[:, lat:],
                          jnp.zeros((D_MODEL, LANES - MLA_NOPE - MLA_ROPE), _F32)], axis=1)
    qd = MLA_NOPE + MLA_ROPE
    wq = jnp.pad(w_q_up.reshape(MLA_Q_RANK, MLA_HEADS, qd), ((0, 0), (0, 0), (0, LANES - qd)))
    kv = w_kv_up.reshape(MLA_KV_RANK, MLA_HEADS, MLA_NOPE + MLA_V)
    wk = jnp.pad(kv[:, :, :MLA_NOPE], ((0, 0), (0, 0), (0, LANES - MLA_NOPE)))
    wv = kv[:, :, MLA_NOPE:]
    return (w1.astype(_BF16), wq.reshape(MLA_Q_RANK, -1).astype(_BF16),
            wk.reshape(MLA_KV_RANK, -1).astype(_BF16), wv.reshape(MLA_KV_RANK, -1).astype(_BF16))


def kernel(x, attn_norm, ev_w_in, ev_lambda_q1, ev_lambda_k1, ev_lambda_q2, ev_lambda_k2, ev_subln, ev_w_out, od_w_in, od_q_norm, od_kv_norm, od_w_q_up, od_w_kv_up, od_w_out, ffn_norm, moe_w_group, moe_b_group, moe_w_expert, moe_b_expert, moe_w_gate, moe_w_up, moe_w_down, final_norm):
    bsz, seq, d = x.shape
    n = bsz * seq
    x2 = x.reshape(n, d)

    lam_init = 0.8 - 0.6 * math.exp(-0.3 * 0)
    qk, vt = _even_proj(x2, attn_norm[0][None, :], ev_w_in[0].astype(_BF16),
                        _even_rope_tables(seq), seq)
    qk3 = qk.reshape(bsz, seq, -1)
    o_a = _moba(qk3, vt).reshape(n, -1)
    lam_rows = jnp.stack([ev_lambda_q1[0], ev_lambda_k1[0], ev_lambda_q2[0], ev_lambda_k2[0]])
    o_b = _diff(qk3, vt, lam_rows, ev_subln[0][None, :], lam_init).reshape(n, -1)
    w_out = ev_w_out[0].astype(_BF16)
    wa, wb = w_out[:o_a.shape[1]], w_out[o_a.shape[1]:]
    wr, br = _router_weights(moe_w_group[0], moe_b_group[0], moe_w_expert[0], moe_b_expert[0])
    x2, rows, cls, rank, counts = _out_proj(x2, [(o_a, wa), (o_b, wb)], ffn_norm[0][None, :],
                                            wr, br)
    ys, dest = _moe(rows, cls, rank, counts, moe_w_gate[0].astype(_BF16),
                    moe_w_up[0].astype(_BF16), moe_w_down[0].astype(_BF16))

    w1, wq, wk, wv = _mla_weights(od_w_in[0], od_w_q_up[0], od_w_kv_up[0])
    tabs = _mla_rope_tables(seq, (MLA_NOPE + MLA_ROPE) ** -0.5 * LOG2E)
    x2, q, k, vt = _mla_proj(dest, x2, ys, attn_norm[1][None, :], w1,
                             od_q_norm[0][None, :], od_kv_norm[0][None, :], wq, wk, wv, tabs, seq)
    o_c = _mla(q.reshape(bsz, seq, -1), k.reshape(bsz, seq, -1), vt)
    wr, br = _router_weights(moe_w_group[1], moe_b_group[1], moe_w_expert[1], moe_b_expert[1])
    x2, rows, cls, rank, counts = _out_proj(x2, [(o_c.reshape(n, -1), od_w_out[0].astype(_BF16))],
                                            ffn_norm[1][None, :], wr, br)
    ys, dest = _moe(rows, cls, rank, counts, moe_w_gate[1].astype(_BF16),
                    moe_w_up[1].astype(_BF16), moe_w_down[1].astype(_BF16))
    out = _final(dest, x2, final_norm[None, :], ys)
    return out.reshape(bsz, seq, d)
```

```python
import functools
import math

import jax
import jax.numpy as jnp
from jax import lax
from jax.experimental import pallas as pl
from jax.experimental.pallas import tpu as pltpu

D_MODEL = 1024
ROPE_THETA = 10000.0
NORM_EPS = 1e-6

HEAD_DIM = 64
MOBA_HEADS = 8
MOBA_BLOCK = 256
MOBA_TOPK = 3
DIFF_HEADS = 4
EVEN_IN = 3072

MLA_HEADS = 16
MLA_Q_RANK = 256
MLA_KV_RANK = 128
MLA_NOPE = 64
MLA_ROPE = 32
MLA_V = 64

N_GROUPS = 4
EXPERTS_PER_GROUP = 8
N_EXPERTS = 32
EXPERT_FF = 256
PAIRS_PER_GROUP = EXPERTS_PER_GROUP * (EXPERTS_PER_GROUP - 1) // 2
N_CLASSES = N_GROUPS * PAIRS_PER_GROUP

LANES = 128
LOG2E = math.log2(math.e)
MASK_BIAS = -1e30
VMEM_LIMIT = 48 * 1024 * 1024

TOKEN_TILE = 512
EXPERT_TILE = 256
ROW_W = D_MODEL + 128
SUBLANES = 8
Q_TILE = 512
Q_GROUP = 4
KV_TILE = 1024
SUB_TILE = 256
assert (Q_TILE * Q_GROUP) % KV_TILE == 0 and KV_TILE % SUB_TILE == 0

_F32 = jnp.float32
_BF16 = jnp.bfloat16


def _dot(a, b):
    return jnp.dot(a, b, preferred_element_type=_F32)


def _rms(x, g):
    return x * lax.rsqrt(jnp.mean(x * x, axis=-1, keepdims=True) + NORM_EPS) * g


def _params(*sem):
    return pltpu.CompilerParams(dimension_semantics=sem, vmem_limit_bytes=VMEM_LIMIT)


def _rope_tables(seq, dim):
    inv = 1.0 / (ROPE_THETA ** (jnp.arange(0, dim, 2, dtype=_F32) / dim))
    ang = jnp.arange(seq, dtype=_F32)[:, None] * inv[None, :]
    return jnp.cos(ang), jnp.sin(ang)


def _even_rope_tables(seq):
    cos, sin = _rope_tables(seq, HEAD_DIM)
    half = HEAD_DIM // 2
    lane = jnp.arange(LANES)
    first = (lane % HEAD_DIM) < half
    c = cos[:, lane % half]
    s = sin[:, lane % half]
    return c, jnp.where(first, -s, 0.0), jnp.where(first, 0.0, s)


def _mla_rope_tables(seq, scale):
    cos, sin = _rope_tables(seq, MLA_ROPE)
    half = MLA_ROPE // 2
    lane = jnp.arange(LANES)
    r = (lane - MLA_NOPE) % half
    in_rope = (lane >= MLA_NOPE) & (lane < MLA_NOPE + MLA_ROPE)
    first = (lane >= MLA_NOPE) & (lane < MLA_NOPE + half)
    second = (lane >= MLA_NOPE + half) & (lane < MLA_NOPE + MLA_ROPE)
    c, s = cos[:, r], sin[:, r]
    cq = jnp.where(lane < MLA_NOPE, 1.0, jnp.where(in_rope, c, 0.0)) * scale
    ck = jnp.where(in_rope, c, 0.0)
    sa = jnp.where(first, -s, 0.0)
    sb = jnp.where(second, s, 0.0)
    return cq, sa * scale, sb * scale, ck, sa, sb


def _rotate(z, c, sa, sb, half):
    return z * c + pltpu.roll(z, LANES - half, 1) * sa + pltpu.roll(z, half, 1) * sb


def _store_transposed(vt_ref, row0, z):
    vt_ref[row0:row0 + LANES, :] = z.T.astype(_BF16)


def _vt_spec(features, tm):
    per_slab = KV_TILE // tm
    return pl.BlockSpec((None, features, tm), lambda i: (i // per_slab, 0, i % per_slab))


def _even_proj_kernel(x_ref, g_ref, w_ref, c_ref, sa_ref, sb_ref, qk_ref, vt_ref):
    h = _rms(x_ref[...], g_ref[...]).astype(_BF16)
    c, sa, sb = c_ref[...], sa_ref[...], sb_ref[...]
    chunk = 512
    blocks = chunk // LANES
    qk_chunk = {0: 0, 1: 1, 3: 2, 4: 3}
    v_chunk = {2: 0, 5: 1}
    for ci in range(EVEN_IN // chunk):
        z = _dot(h, w_ref[:, ci * chunk:(ci + 1) * chunk])
        for b in range(blocks):
            zz = z[:, b * LANES:(b + 1) * LANES]
            if ci in v_chunk:
                _store_transposed(vt_ref, (v_chunk[ci] * blocks + b) * LANES, zz)
                continue
            zz = _rotate(zz, c, sa, sb, HEAD_DIM // 2)
            if ci in (0, 3):
                zz = zz * (HEAD_DIM ** -0.5 * LOG2E)
            col = (qk_chunk[ci] * blocks + b) * LANES
            qk_ref[:, col:col + LANES] = zz.astype(_BF16)


def _even_proj(x2, g, w, tabs, seq):
    n = x2.shape[0]
    tm = TOKEN_TILE
    spb = seq // tm
    tab_spec = pl.BlockSpec((tm, LANES), lambda i: (i % spb, 0))
    qk_w = 4 * 512
    v_w = 2 * 512
    return pl.pallas_call(
        _even_proj_kernel,
        grid=(n // tm,),
        in_specs=[
            pl.BlockSpec((tm, D_MODEL), lambda i: (i, 0)),
            pl.BlockSpec((1, D_MODEL), lambda i: (0, 0)),
            pl.BlockSpec((D_MODEL, EVEN_IN), lambda i: (0, 0)),
            tab_spec, tab_spec, tab_spec,
        ],
        out_specs=[pl.BlockSpec((tm, qk_w), lambda i: (i, 0)), _vt_spec(v_w, tm)],
        out_shape=[jax.ShapeDtypeStruct((n, qk_w), _BF16),
                   jax.ShapeDtypeStruct((n // KV_TILE, v_w, KV_TILE), _BF16)],
        compiler_params=_params("parallel"),
        name="even_proj",
    )(x2, g, w, *tabs)


class _AttnRefs:
    def __init__(self, s_ref, pm_ref, sd_ref, pmd_ref, m_ref, l_ref, acc_ref):
        self.main = [(s_ref, pm_ref, 0), (s_ref, pm_ref, 1)]
        self.diag = [(sd_ref, pmd_ref, 0), (sd_ref, pmd_ref, 1)]
        self.m, self.l, self.acc = m_ref, l_ref, acc_ref


def _attn_scratch(dv, tq):
    scores = pltpu.VMEM((2, 2, KV_TILE, tq), _F32)
    maxima = pltpu.VMEM((2, 2, 8, tq), _F32)
    return [scores, maxima, scores, maxima,
            pltpu.VMEM((2, 1, tq), _F32),
            pltpu.VMEM((2, 1, tq), _F32),
            pltpu.VMEM((2, dv, tq), _F32)]


def _kv_rows(c):
    return pl.ds(pl.multiple_of(c * KV_TILE, KV_TILE), KV_TILE)


def _sub_rows(c, r):
    return pl.ds(pl.multiple_of(c * KV_TILE + r * SUB_TILE, SUB_TILE), SUB_TILE)


def _fold8(x, op):
    return op(x.reshape(x.shape[0] // 8, 8, x.shape[1]), axis=0)


def _stage(refs, value, softmax_of=None, scores_of=None):
    chains = range(2)
    n_sub = KV_TILE // SUB_TILE
    n_in = n_out = 0
    if softmax_of is not None:
        (s_in, pm_in, b_in), c, n_in = softmax_of
        m_new, alpha, acc, lsum = [], [], [], [None, None]
        for t in chains:
            m_prev = refs.m[t]
            m_new.append(jnp.maximum(m_prev, jnp.max(pm_in[b_in, t], axis=0, keepdims=True)))
            alpha.append(jnp.exp2(m_prev - m_new[t]))
            acc.append(alpha[t] * refs.acc[t])
    if scores_of is not None:
        (s_out, pm_out, b_out), c_out, score, mask, n_out = scores_of
        pm = [None, None]
    for r in range(n_sub):
        rows = slice(r * SUB_TILE, (r + 1) * SUB_TILE)
        for t in chains:
            if r < n_out:
                s = score(t, c_out, r)
                if mask is not None:
                    s = mask(s, r)
                s_out[b_out, t, rows, :] = s
                part = _fold8(s, jnp.max)
                pm[t] = part if pm[t] is None else jnp.maximum(pm[t], part)
            if r < n_in:
                p = jnp.exp2(s_in[b_in, t, rows, :] - m_new[t])
                part = _fold8(p, jnp.sum)
                lsum[t] = part if lsum[t] is None else lsum[t] + part
                acc[t] = acc[t] + _dot(value(t, c, r), p.astype(_BF16))
    for t in chains:
        if scores_of is not None:
            pm_out[b_out, t] = pm[t]
        if softmax_of is not None:
            refs.m[t] = m_new[t]
            refs.l[t] = alpha[t] * refs.l[t] + jnp.sum(lsum[t], axis=0, keepdims=True)
            refs.acc[t] = acc[t]


def _sweep(refs, first_qi, make_score, value, finalize):
    tq = refs.m.shape[-1]
    n_sub = KV_TILE // SUB_TILE
    m0, m1 = refs.main
    delta = (lax.broadcasted_iota(jnp.int32, (SUB_TILE, tq), 0)
             - lax.broadcasted_iota(jnp.int32, (SUB_TILE, tq), 1))

    def plan(j):
        last = ((first_qi + j) * tq) // KV_TILE
        start = (j * tq) % KV_TILE
        n_diag = min(n_sub, (start + tq - 1) // SUB_TILE + 1)

        def causal(s, r):
            if (r + 1) * SUB_TILE - 1 <= start:
                return s
            return jnp.where(delta <= start - r * SUB_TILE, s, -jnp.inf)

        return last, n_diag, (refs.diag[j % 2], last, make_score(j), causal, n_diag)

    plans = [plan(j) for j in range(Q_GROUP)]
    _stage(refs, value, scores_of=plans[0][2])

    for j, (last, n_diag, (diag, _, score, _, _)) in enumerate(plans):
        refs.m[...] = jnp.full(refs.m.shape, -jnp.inf, _F32)
        refs.l[...] = jnp.zeros(refs.l.shape, _F32)
        refs.acc[...] = jnp.zeros(refs.acc.shape, _F32)
        odd = last % 2

        def below(buf, tile):
            return (buf, tile, score, None, n_sub)

        @pl.when(odd == 1)
        def _():
            _stage(refs, value, softmax_of=(diag, last, n_diag), scores_of=below(m1, 0))

        @pl.when(jnp.logical_and(odd == 0, last >= 2))
        def _():
            _stage(refs, value, softmax_of=(diag, last, n_diag), scores_of=below(m0, 0))
            _stage(refs, value, softmax_of=(m0, 0, n_sub), scores_of=below(m1, 1))

        base = 1 - odd

        def pair(k, carry):
            c = base + 2 * k
            _stage(refs, value, softmax_of=(m1, c, n_sub), scores_of=below(m0, c + 1))
            _stage(refs, value, softmax_of=(m0, c + 1, n_sub), scores_of=below(m1, c + 2))
            return carry

        lax.fori_loop(0, lax.shift_right_arithmetic(last - 1 - base, 1), pair, 0)

        nxt = plans[j + 1][2] if j + 1 < Q_GROUP else None

        @pl.when(last == 0)
        def _():
            _stage(refs, value, softmax_of=(diag, last, n_diag), scores_of=nxt)

        @pl.when(last > 0)
        def _():
            _stage(refs, value, softmax_of=(m1, last - 1, n_sub), scores_of=nxt)

        finalize(j)


def _transposed(q):
    return q.astype(_F32).T


def _moba_kernel(q_ref, k_ref, vt_ref, o_ref, km_ref, ka_ref, *scratch, nb, topk):
    step = pl.program_id(2)
    tq = Q_TILE
    half = HEAD_DIM // 2

    @pl.when(step == 0)
    def _():
        kf = k_ref[...].astype(_F32)
        means = jnp.sum(kf.reshape(nb, MOBA_BLOCK, LANES), axis=1) * (1.0 / MOBA_BLOCK)
        lane_b = lax.broadcasted_iota(jnp.int32, (nb, LANES), 1)
        km_ref[...] = jnp.zeros(km_ref.shape, _F32)
        km_ref[0:nb, :] = jnp.where(lane_b < HEAD_DIM, 0.0, means)
        km_ref[HEAD_DIM:HEAD_DIM + nb, :] = jnp.where(lane_b < HEAD_DIM, means, 0.0)

        lane = lax.broadcasted_iota(jnp.int32, (KV_TILE, LANES), 1)
        blk = lax.broadcasted_iota(jnp.int32, (KV_TILE, LANES), 0) // MOBA_BLOCK

        def build(c, carry):
            rows = _kv_rows(c)
            k = k_ref[rows, :].astype(_F32)
            b = blk + c * (KV_TILE // MOBA_BLOCK)
            ka_ref[0, rows, :] = jnp.where(lane < HEAD_DIM, k,
                                           jnp.where(lane == b + HEAD_DIM, 1.0, 0.0)).astype(_BF16)
            ka_ref[1, rows, :] = jnp.where(lane < HEAD_DIM,
                                           jnp.where(lane == b, 1.0, 0.0), k).astype(_BF16)
            return carry

        lax.fori_loop(0, k_ref.shape[0] // KV_TILE, build, 0)

    row = lax.broadcasted_iota(jnp.int32, (half, tq), 0).astype(_F32)
    z_half = jnp.zeros((half, tq), _F32)

    def make_score(j):
        qt = _transposed(q_ref[j * tq:(j + 1) * tq, :])
        gate = _dot(km_ref[...].astype(_BF16), qt.astype(_BF16))
        qpos = (step * Q_GROUP + j) * tq + lax.broadcasted_iota(jnp.int32, (half, tq), 1)
        own = (qpos // MOBA_BLOCK).astype(_F32)

        def block_bias(base):
            g = jnp.where(row < own, gate[base:base + half], -jnp.inf)
            bias = jnp.where(row == own, 0.0, MASK_BIAS)
            for _ in range(topk):
                mx = jnp.max(g, axis=0, keepdims=True)
                hit = (g == mx) & (mx > -jnp.inf)
                idx = jnp.min(jnp.where(hit, row, 4.0 * LANES), axis=0, keepdims=True)
                pick = row == idx
                bias = jnp.where(pick, 0.0, bias)
                g = jnp.where(pick, -jnp.inf, g)
            return bias

        qs = (jnp.concatenate([qt[:HEAD_DIM], block_bias(HEAD_DIM), z_half], axis=0).astype(_BF16),
              jnp.concatenate([block_bias(0), z_half, qt[HEAD_DIM:]], axis=0).astype(_BF16))
        return lambda t, c, r: _dot(ka_ref[t, _sub_rows(c, r), :], qs[t])

    def value(t, c, r):
        return vt_ref[c, t * HEAD_DIM:(t + 1) * HEAD_DIM, r * SUB_TILE:(r + 1) * SUB_TILE]

    refs = _AttnRefs(*scratch)

    def finalize(j):
        ot = jnp.concatenate([refs.acc[0] / refs.l[0], refs.acc[1] / refs.l[1]], axis=0)
        o_ref[j * tq:(j + 1) * tq, :] = ot.T.astype(o_ref.dtype)

    _sweep(refs, step * Q_GROUP, make_score, value, finalize)


def _moba(qk3, vt):
    bsz, seq, _ = qk3.shape
    tq = Q_TILE * Q_GROUP
    assert Q_TILE % MOBA_BLOCK == 0 and seq % KV_TILE == 0 and seq % tq == 0
    nb = seq // MOBA_BLOCK
    assert nb <= HEAD_DIM // 2
    topk = max(1, min(MOBA_TOPK, nb - 1))
    pairs = MOBA_HEADS // 2
    return pl.pallas_call(
        functools.partial(_moba_kernel, nb=nb, topk=topk),
        grid=(bsz, pairs, seq // tq),
        in_specs=[
            pl.BlockSpec((None, tq, LANES), lambda b, p, i: (b, i, p)),
            pl.BlockSpec((None, seq, LANES), lambda b, p, i: (b, 0, pairs + p)),
            pl.BlockSpec((seq // KV_TILE, LANES, KV_TILE), lambda b, p, i: (b, p, 0)),
        ],
        out_specs=pl.BlockSpec((None, tq, LANES), lambda b, p, i: (b, i, p)),
        out_shape=jax.ShapeDtypeStruct((bsz, seq, pairs * LANES), _BF16),
        scratch_shapes=[pltpu.VMEM((LANES, LANES), _F32), pltpu.VMEM((2, seq, LANES), _BF16)]
                       + _attn_scratch(HEAD_DIM, Q_TILE),
        compiler_params=_params("parallel", "parallel", "arbitrary"),
        name="moba_attn",
    )(qk3, qk3, vt)


def _diff_kernel(q_ref, k_ref, vt_ref, lam_ref, g_ref, o_ref, *scratch, lam_init):
    step = pl.program_id(2)
    tq = Q_TILE
    z_head = jnp.zeros((HEAD_DIM, tq), _F32)

    def make_score(j):
        qt = _transposed(q_ref[j * tq:(j + 1) * tq, :])
        qd = (jnp.concatenate([qt[:HEAD_DIM], z_head], axis=0).astype(_BF16),
              jnp.concatenate([z_head, qt[HEAD_DIM:]], axis=0).astype(_BF16))
        return lambda t, c, r: _dot(k_ref[_sub_rows(c, r), :], qd[t])

    def value(t, c, r):
        return vt_ref[c, :, r * SUB_TILE:(r + 1) * SUB_TILE]

    lv = lam_ref[...]
    lam = (jnp.exp(jnp.sum(lv[0:1] * lv[1:2], axis=1, keepdims=True))
           - jnp.exp(jnp.sum(lv[2:3] * lv[3:4], axis=1, keepdims=True)) + lam_init)
    refs = _AttnRefs(*scratch)

    def finalize(j):
        ot = refs.acc[0] / refs.l[0] - lam * (refs.acc[1] / refs.l[1])
        o_ref[j * tq:(j + 1) * tq, :] = (_rms(ot.T, g_ref[...]) * (1.0 - lam_init)).astype(o_ref.dtype)

    _sweep(refs, step * Q_GROUP, make_score, value, finalize)


def _diff(qk3, vt, lam_rows, subln, lam_init):
    bsz, seq, _ = qk3.shape
    tq = Q_TILE * Q_GROUP
    base = 2 * MOBA_HEADS * HEAD_DIM // LANES
    vbase = MOBA_HEADS * HEAD_DIM // LANES
    return pl.pallas_call(
        functools.partial(_diff_kernel, lam_init=lam_init),
        grid=(bsz, DIFF_HEADS, seq // tq),
        in_specs=[
            pl.BlockSpec((None, tq, LANES), lambda b, h, i: (b, i, base + h)),
            pl.BlockSpec((None, seq, LANES), lambda b, h, i: (b, 0, base + DIFF_HEADS + h)),
            pl.BlockSpec((seq // KV_TILE, LANES, KV_TILE), lambda b, h, i: (b, vbase + h, 0)),
            pl.BlockSpec((4, HEAD_DIM), lambda b, h, i: (0, 0)),
            pl.BlockSpec((1, LANES), lambda b, h, i: (0, 0)),
        ],
        out_specs=pl.BlockSpec((None, tq, LANES), lambda b, h, i: (b, i, h)),
        out_shape=jax.ShapeDtypeStruct((bsz, seq, DIFF_HEADS * LANES), _BF16),
        scratch_shapes=_attn_scratch(LANES, Q_TILE),
        compiler_params=_params("parallel", "parallel", "arbitrary"),
        name="diff_attn",
    )(qk3, qk3, vt, lam_rows, subln)


def _mla_proj_kernel(dest_ref, x_ref, ys_ref,
                     g_ref, w1_ref, qg_ref, kg_ref, wq_ref, wk_ref, wv_ref,
                     cq_ref, saq_ref, sbq_ref, ck_ref, sak_ref, sbk_ref,
                     x_out, q_out, k_out, vt_out, buf_ref, sem):
    x = x_ref[...] + _gather_rows(dest_ref, ys_ref, buf_ref, sem)
    x_out[...] = x
    h = _rms(x, g_ref[...]).astype(_BF16)
    z = _dot(h, w1_ref[...])
    cq = _rms(z[:, :MLA_Q_RANK], qg_ref[...]).astype(_BF16)
    ckv = _rms(z[:, MLA_Q_RANK:MLA_Q_RANK + MLA_KV_RANK], kg_ref[...]).astype(_BF16)
    kr = z[:, MLA_Q_RANK + MLA_KV_RANK:]
    half = MLA_ROPE // 2
    kr = _rotate(kr, ck_ref[...], sak_ref[...], sbk_ref[...], half)
    qf = _dot(cq, wq_ref[...])
    kf = _dot(ckv, wk_ref[...])
    cqt, saq, sbq = cq_ref[...], saq_ref[...], sbq_ref[...]
    for hd in range(MLA_HEADS):
        sl = slice(hd * LANES, (hd + 1) * LANES)
        q_out[:, sl] = _rotate(qf[:, sl], cqt, saq, sbq, half).astype(_BF16)
        k_out[:, sl] = (kf[:, sl] + kr).astype(_BF16)
    vf = _dot(ckv, wv_ref[...])
    for b in range(MLA_HEADS * MLA_V // LANES):
        _store_transposed(vt_out, b * LANES, vf[:, b * LANES:(b + 1) * LANES])


def _mla_proj(dest, x2, ys, g, w1, qg, kg, wq, wk, wv, tabs, seq):
    n = x2.shape[0]
    tm = TOKEN_TILE
    spb = seq // tm
    hw = MLA_HEADS * LANES
    vw = MLA_HEADS * MLA_V
    tab_spec = pl.BlockSpec((tm, LANES), lambda i: (i % spb, 0))

    def full(a):
        return pl.BlockSpec(a.shape, lambda i: (0,) * a.ndim)

    return pl.pallas_call(
        _mla_proj_kernel,
        grid=(n // tm,),
        in_specs=[_index_spec(tm),
                  pl.BlockSpec((tm, D_MODEL), lambda i: (i, 0)),
                  pl.BlockSpec(memory_space=pl.ANY),
                  full(g), full(w1), full(qg), full(kg), full(wq), full(wk), full(wv)]
                 + [tab_spec] * 6,
        out_specs=[pl.BlockSpec((tm, D_MODEL), lambda i: (i, 0)),
                   pl.BlockSpec((tm, hw), lambda i: (i, 0)),
                   pl.BlockSpec((tm, hw), lambda i: (i, 0)),
                   _vt_spec(vw, tm)],
        out_shape=[jax.ShapeDtypeStruct((n, D_MODEL), _F32),
                   jax.ShapeDtypeStruct((n, hw), _BF16),
                   jax.ShapeDtypeStruct((n, hw), _BF16),
                   jax.ShapeDtypeStruct((n // KV_TILE, vw, KV_TILE), _BF16)],
        scratch_shapes=_gather_scratch(tm),
        compiler_params=_params("arbitrary"),
        name="mla_proj",
    )(dest, x2, ys, g, w1, qg, kg, wq, wk, wv, *tabs)


def _mla_kernel(q_ref, k_ref, vt_ref, o_ref, *scratch):
    step = pl.program_id(2)
    tq = Q_TILE

    def make_score(j):
        q = q_ref[j * tq:(j + 1) * tq, :]
        qd = tuple(_transposed(q[:, t * LANES:(t + 1) * LANES]).astype(_BF16) for t in range(2))
        return lambda t, c, r: _dot(k_ref[_sub_rows(c, r), t * LANES:(t + 1) * LANES], qd[t])

    def value(t, c, r):
        return vt_ref[c, t * MLA_V:(t + 1) * MLA_V, r * SUB_TILE:(r + 1) * SUB_TILE]

    refs = _AttnRefs(*scratch)

    def finalize(j):
        ot = jnp.concatenate([refs.acc[0] / refs.l[0], refs.acc[1] / refs.l[1]], axis=0)
        o_ref[j * tq:(j + 1) * tq, :] = ot.T.astype(o_ref.dtype)

    _sweep(refs, step * Q_GROUP, make_score, value, finalize)


def _mla(q3, k3, vt):
    bsz, seq, _ = q3.shape
    tq = Q_TILE * Q_GROUP
    pairs = MLA_HEADS // 2
    return pl.pallas_call(
        _mla_kernel,
        grid=(bsz, pairs, seq // tq),
        in_specs=[
            pl.BlockSpec((None, tq, 2 * LANES), lambda b, p, i: (b, i, p)),
            pl.BlockSpec((None, seq, 2 * LANES), lambda b, p, i: (b, 0, p)),
            pl.BlockSpec((seq // KV_TILE, LANES, KV_TILE), lambda b, p, i: (b, p, 0)),
        ],
        out_specs=pl.BlockSpec((None, tq, LANES), lambda b, p, i: (b, i, p)),
        out_shape=jax.ShapeDtypeStruct((bsz, seq, pairs * LANES), _BF16),
        scratch_shapes=_attn_scratch(MLA_V, Q_TILE),
        compiler_params=_params("parallel", "parallel", "arbitrary"),
        name="mla_attn",
    )(q3, k3, vt)


def _route(logits):
    lane = lax.broadcasted_iota(jnp.int32, logits.shape, 1).astype(_F32)
    big = 4.0 * LANES
    is_group = (lane >= N_EXPERTS) & (lane < N_EXPERTS + N_GROUPS)
    gl = jnp.where(is_group, logits, -jnp.inf)
    gmax = jnp.max(gl, axis=1, keepdims=True)
    grp = jnp.min(jnp.where(gl == gmax, lane, big), axis=1, keepdims=True) - N_EXPERTS
    p_group = 1.0 / jnp.sum(jnp.exp(gl - gmax), axis=1, keepdims=True)
    in_group = (lane >= grp * EXPERTS_PER_GROUP) & (lane < (grp + 1) * EXPERTS_PER_GROUP)
    el = jnp.where(in_group, logits, -jnp.inf)
    e1 = jnp.max(el, axis=1, keepdims=True)
    i1 = jnp.min(jnp.where(el == e1, lane, big), axis=1, keepdims=True)
    el2 = jnp.where(lane == i1, -jnp.inf, el)
    e2 = jnp.max(el2, axis=1, keepdims=True)
    i2 = jnp.min(jnp.where(el2 == e2, lane, big), axis=1, keepdims=True)
    r = jnp.exp(e2 - e1)
    w1 = p_group / (1.0 + r)
    w2 = w1 * r
    first = i1 < i2
    a = jnp.minimum(i1, i2) - grp * EXPERTS_PER_GROUP
    b = jnp.maximum(i1, i2) - grp * EXPERTS_PER_GROUP
    cls = grp * PAIRS_PER_GROUP + a * (2 * EXPERTS_PER_GROUP - 1 - a) * 0.5 + (b - a - 1.0)
    return jnp.where(first, w1, w2), jnp.where(first, w2, w1), cls


def _lane_dense(col, eye):
    t = col.shape[0]
    return jnp.sum(jnp.where(eye, col, 0.0).reshape(t // LANES, LANES, LANES), axis=1)


def _out_proj_kernel(*refs, n_mix):
    x_ref = refs[0]
    mix = refs[1:1 + 2 * n_mix]
    g_ref, wr_ref, br_ref = refs[1 + 2 * n_mix:4 + 2 * n_mix]
    x_out, row_out, cls_out, rank_out, count_out, count_ref = refs[4 + 2 * n_mix:]

    @pl.when(pl.program_id(0) == 0)
    def _():
        count_ref[...] = jnp.zeros(count_ref.shape, _F32)

    x = x_ref[...]
    for i in range(n_mix):
        x = x + _dot(mix[2 * i][...], mix[2 * i + 1][...])
    x_out[...] = x
    t = _rms(x, g_ref[...])
    w_lo, w_hi, cls = _route(_dot(t.astype(_BF16), wr_ref[...]) + br_ref[...])

    tm = x.shape[0]
    lane = lax.broadcasted_iota(jnp.int32, (tm, LANES), 1)
    row_out[:, :D_MODEL] = t
    row_out[:, D_MODEL:] = jnp.where(lane == 0, w_lo, 0.0) + jnp.where(lane == 1, w_hi, 0.0)

    onehot = jnp.where(lane.astype(_F32) == cls, 1.0, 0.0)
    earlier = (lax.broadcasted_iota(jnp.int32, (tm, tm), 1)
               < lax.broadcasted_iota(jnp.int32, (tm, tm), 0))
    before = _dot(jnp.where(earlier, 1.0, 0.0).astype(_BF16), onehot.astype(_BF16))
    rank = jnp.sum(onehot * (before + count_ref[...]), axis=1, keepdims=True)
    count_ref[...] += jnp.sum(onehot, axis=0, keepdims=True)
    count_out[...] = count_ref[...].astype(jnp.int32)
    eye = lax.broadcasted_iota(jnp.int32, (tm, LANES), 0) % LANES == lane
    cls_out[...] = _lane_dense(cls, eye).astype(jnp.int32)
    rank_out[...] = _lane_dense(rank, eye).astype(jnp.int32)


def _out_proj(x2, mixes, g, wr, br):
    n = x2.shape[0]
    tm = TOKEN_TILE
    in_specs = [pl.BlockSpec((tm, D_MODEL), lambda i: (i, 0))]
    args = [x2]
    for o, w in mixes:
        in_specs.append(pl.BlockSpec((tm, o.shape[1]), lambda i: (i, 0)))
        in_specs.append(pl.BlockSpec(w.shape, lambda i: (0, 0)))
        args += [o, w]
    for a in (g, wr, br):
        in_specs.append(pl.BlockSpec(a.shape, lambda i: (0, 0)))
        args.append(a)
    idx_spec = pl.BlockSpec((None, tm // LANES, LANES), lambda i: (i, 0, 0))
    idx_shape = jax.ShapeDtypeStruct((n // tm, tm // LANES, LANES), jnp.int32)
    return pl.pallas_call(
        functools.partial(_out_proj_kernel, n_mix=len(mixes)),
        grid=(n // tm,),
        in_specs=in_specs,
        out_specs=[pl.BlockSpec((tm, D_MODEL), lambda i: (i, 0)),
                   pl.BlockSpec((tm, ROW_W), lambda i: (i, 0)),
                   idx_spec, idx_spec,
                   pl.BlockSpec((1, LANES), lambda i: (0, 0))],
        out_shape=[jax.ShapeDtypeStruct((n, D_MODEL), _F32),
                   jax.ShapeDtypeStruct((n, ROW_W), _F32),
                   idx_shape, idx_shape,
                   jax.ShapeDtypeStruct((1, LANES), jnp.int32)],
        scratch_shapes=[pltpu.VMEM((1, LANES), _F32)],
        compiler_params=_params("arbitrary"),
        name="out_proj_route",
    )(*args)


def _class_experts():
    pairs = [(a, b) for a in range(EXPERTS_PER_GROUP) for b in range(a + 1, EXPERTS_PER_GROUP)]
    lo = [g * EXPERTS_PER_GROUP + a for g in range(N_GROUPS) for a, _ in pairs]
    hi = [g * EXPERTS_PER_GROUP + b for g in range(N_GROUPS) for _, b in pairs]
    return jnp.asarray(lo, jnp.int32), jnp.asarray(hi, jnp.int32)


def _tile_plan(counts, max_tiles):
    tiles = (counts + EXPERT_TILE - 1) // EXPERT_TILE
    ends = jnp.cumsum(tiles)
    offsets = (ends - tiles) * EXPERT_TILE
    tile = jnp.arange(max_tiles, dtype=jnp.int32)
    tile_cls = jnp.sum((ends[None, :] <= tile[:, None]).astype(jnp.int32), axis=1)
    tile_cls = jnp.minimum(tile_cls, N_CLASSES - 1)
    lo, hi = _class_experts()
    return offsets.astype(jnp.int32), lo[tile_cls], hi[tile_cls], ends[-1:].astype(jnp.int32)


def _sorted_rows_kernel(cls_ref, rank_ref, off_ref, dest_ref):
    table = jnp.broadcast_to(off_ref[...], cls_ref.shape)
    dest_ref[...] = jnp.take_along_axis(table, cls_ref[...], axis=1) + rank_ref[...]


def _sorted_rows(cls, rank, offsets):
    spec = pl.BlockSpec((None,) + cls.shape[1:], lambda i: (i, 0, 0))
    return pl.pallas_call(
        _sorted_rows_kernel,
        grid=(cls.shape[0],),
        in_specs=[spec, spec, pl.BlockSpec((1, LANES), lambda i: (0, 0))],
        out_specs=spec,
        out_shape=jax.ShapeDtypeStruct(cls.shape, jnp.int32),
        compiler_params=_params("parallel"),
        name="moe_sorted_rows",
    )(cls, rank, offsets[None, :])


def _row_copies(dest_ref, copy):
    tiles_per_row = LANES // SUBLANES

    def each(method):
        for a in range(dest_ref.shape[0]):
            def body(g, carry, a=a):
                for j in range(SUBLANES):
                    d = dest_ref[a, g * SUBLANES + j]
                    getattr(copy(a * tiles_per_row + g, j, d), method)()
                return carry

            lax.fori_loop(0, tiles_per_row, body, 0)

    each("start")
    each("wait")


def _dispatch_kernel(dest_ref, row_ref, xs_in_ref, xs_ref, sem):
    del xs_in_ref

    def copy(tile, j, d):
        return pltpu.make_async_copy(row_ref.at[tile, pl.ds(j, 1), :],
                                     xs_ref.at[pl.ds(d, 1), :], sem)

    _row_copies(dest_ref, copy)


def _index_spec(tm):
    return pl.BlockSpec((None, tm // LANES, LANES), lambda i: (i, 0, 0), memory_space=pltpu.SMEM)


def _dispatch(dest, rows, max_tiles):
    n = rows.shape[0]
    tm = TOKEN_TILE
    xs0 = jnp.zeros((max_tiles * EXPERT_TILE, ROW_W), _F32)
    return pl.pallas_call(
        _dispatch_kernel,
        grid=(n // tm,),
        in_specs=[_index_spec(tm),
                  pl.BlockSpec((tm // SUBLANES, SUBLANES, ROW_W), lambda i: (i, 0, 0)),
                  pl.BlockSpec(memory_space=pl.ANY)],
        out_specs=pl.BlockSpec(memory_space=pl.ANY),
        out_shape=jax.ShapeDtypeStruct(xs0.shape, _F32),
        scratch_shapes=[pltpu.SemaphoreType.DMA(())],
        input_output_aliases={2: 0},
        compiler_params=_params("arbitrary"),
        name="moe_dispatch",
    )(dest, rows.reshape(n // SUBLANES, SUBLANES, ROW_W), xs0)


def _expert_kernel(lo_ref, hi_ref, used_ref, xs_ref, wg0, wu0, wd0, wg1, wu1, wd1, ys_ref):
    del lo_ref, hi_ref
    i = pl.program_id(0)

    @pl.when(i < used_ref[0])
    def _():
        xs = xs_ref[...]
        x = xs[:, :D_MODEL].astype(_BF16)
        gates = xs[:, D_MODEL:]
        ups = [(_dot(x, wg[...]), _dot(x, wu[...])) for wg, wu in ((wg0, wu0), (wg1, wu1))]
        acts = [(hg * (1.0 / (1.0 + jnp.exp(-hg))) * hu * gates[:, k:k + 1]).astype(_BF16)
                for k, (hg, hu) in enumerate(ups)]
        ys_ref[...] = _dot(acts[0], wd0[...]) + _dot(acts[1], wd1[...])

    @pl.when(i >= used_ref[0])
    def _():
        ys_ref[...] = jnp.zeros(ys_ref.shape, _F32)


def _experts(xs, tile_lo, tile_hi, used, wg, wu, wd):
    max_tiles = xs.shape[0] // EXPERT_TILE
    up = (None, D_MODEL, EXPERT_FF)
    down = (None, EXPERT_FF, D_MODEL)
    by_lo = lambda i, lo, hi, used: (lo[i], 0, 0)
    by_hi = lambda i, lo, hi, used: (hi[i], 0, 0)
    grid_spec = pltpu.PrefetchScalarGridSpec(
        num_scalar_prefetch=3,
        grid=(max_tiles,),
        in_specs=[pl.BlockSpec((EXPERT_TILE, ROW_W),
                               lambda i, lo, hi, used: (jnp.minimum(i, used[0] - 1), 0)),
                  pl.BlockSpec(up, by_lo), pl.BlockSpec(up, by_lo), pl.BlockSpec(down, by_lo),
                  pl.BlockSpec(up, by_hi), pl.BlockSpec(up, by_hi), pl.BlockSpec(down, by_hi)],
        out_specs=pl.BlockSpec((EXPERT_TILE, D_MODEL), lambda i, lo, hi, used: (i, 0)),
    )
    return pl.pallas_call(
        _expert_kernel,
        grid_spec=grid_spec,
        out_shape=jax.ShapeDtypeStruct((xs.shape[0], D_MODEL), _F32),
        compiler_params=_params("arbitrary"),
        name="moe_experts",
    )(tile_lo, tile_hi, used, xs, wg, wu, wd, wg, wu, wd)


def _gather_rows(dest_ref, ys_ref, buf_ref, sem):
    def copy(tile, j, d):
        return pltpu.make_async_copy(ys_ref.at[pl.ds(d, 1), :],
                                     buf_ref.at[tile, pl.ds(j, 1), :], sem)

    _row_copies(dest_ref, copy)
    return buf_ref[...].reshape(buf_ref.shape[0] * SUBLANES, buf_ref.shape[2])


def _gather_scratch(tm):
    return [pltpu.VMEM((tm // SUBLANES, SUBLANES, D_MODEL), _F32), pltpu.SemaphoreType.DMA(())]


def _final_kernel(dest_ref, x_ref, g_ref, ys_ref, o_ref, buf_ref, sem):
    o_ref[...] = _rms(x_ref[...] + _gather_rows(dest_ref, ys_ref, buf_ref, sem), g_ref[...])


def _final(dest, x2, g, ys):
    n = x2.shape[0]
    tm = TOKEN_TILE
    return pl.pallas_call(
        _final_kernel,
        grid=(n // tm,),
        in_specs=[_index_spec(tm),
                  pl.BlockSpec((tm, D_MODEL), lambda i: (i, 0)),
                  pl.BlockSpec((1, D_MODEL), lambda i: (0, 0)),
                  pl.BlockSpec(memory_space=pl.ANY)],
        out_specs=pl.BlockSpec((tm, D_MODEL), lambda i: (i, 0)),
        out_shape=jax.ShapeDtypeStruct((n, D_MODEL), _F32),
        scratch_shapes=_gather_scratch(tm),
        compiler_params=_params("arbitrary"),
        name="moe_combine_norm",
    )(dest, x2, g, ys)


def _moe(rows, cls, rank, counts, wg, wu, wd):
    n = rows.shape[0]
    max_tiles = n // EXPERT_TILE + N_CLASSES
    offsets, tile_lo, tile_hi, used = _tile_plan(counts[0], max_tiles)
    dest = _sorted_rows(cls, rank, offsets)
    xs = _dispatch(dest, rows, max_tiles)
    return _experts(xs, tile_lo, tile_hi, used, wg, wu, wd), dest


def _router_weights(w_group, b_group, w_expert, b_expert):
    pad = LANES - N_EXPERTS - N_GROUPS
    wr = jnp.concatenate([w_expert, w_group, jnp.zeros((D_MODEL, pad), _F32)], axis=1)
    br = jnp.concatenate([b_expert, b_group, jnp.zeros((pad,), _F32)])[None, :]
    return wr.astype(_BF16), br


def _mla_weights(w_in, w_q_up, w_kv_up):
    lat = MLA_Q_RANK + MLA_KV_RANK
    w1 = jnp.concatenate([w_in[:, :lat], jnp.zeros((D_MODEL, MLA_NOPE), _F32), w_in[:, lat:],
                          jnp.zeros((D_MODEL, LANES - MLA_NOPE - MLA_ROPE), _F32)], axis=1)
    qd = MLA_NOPE + MLA_ROPE
    wq = jnp.pad(w_q_up.reshape(MLA_Q_RANK, MLA_HEADS, qd), ((0, 0), (0, 0), (0, LANES - qd)))
    kv = w_kv_up.reshape(MLA_KV_RANK, MLA_HEADS, MLA_NOPE + MLA_V)
    wk = jnp.pad(kv[:, :, :MLA_NOPE], ((0, 0), (0, 0), (0, LANES - MLA_NOPE)))
    wv = kv[:, :, MLA_NOPE:]
    return (w1.astype(_BF16), wq.reshape(MLA_Q_RANK, -1).astype(_BF16),
            wk.reshape(MLA_KV_RANK, -1).astype(_BF16), wv.reshape(MLA_KV_RANK, -1).astype(_BF16))


def kernel(x, attn_norm, ev_w_in, ev_lambda_q1, ev_lambda_k1, ev_lambda_q2, ev_lambda_k2, ev_subln, ev_w_out, od_w_in, od_q_norm, od_kv_norm, od_w_q_up, od_w_kv_up, od_w_out, ffn_norm, moe_w_group, moe_b_group, moe_w_expert, moe_b_expert, moe_w_gate, moe_w_up, moe_w_down, final_norm):
    bsz, seq, d = x.shape
    n = bsz * seq
    x2 = x.reshape(n, d)

    lam_init = 0.8 - 0.6 * math.exp(-0.3 * 0)
    qk, vt = _even_proj(x2, attn_norm[0][None, :], ev_w_in[0].astype(_BF16),
                        _even_rope_tables(seq), seq)
    qk3 = qk.reshape(bsz, seq, -1)
    o_a = _moba(qk3, vt).reshape(n, -1)
    lam_rows = jnp.stack([ev_lambda_q1[0], ev_lambda_k1[0], ev_lambda_q2[0], ev_lambda_k2[0]])
    o_b = _diff(qk3, vt, lam_rows, ev_subln[0][None, :], lam_init).reshape(n, -1)
    w_out = ev_w_out[0].astype(_BF16)
    wa, wb = w_out[:o_a.shape[1]], w_out[o_a.shape[1]:]
    wr, br = _router_weights(moe_w_group[0], moe_b_group[0], moe_w_expert[0], moe_b_expert[0])
    x2, rows, cls, rank, counts = _out_proj(x2, [(o_a, wa), (o_b, wb)], ffn_norm[0][None, :],
                                            wr, br)
    ys, dest = _moe(rows, cls, rank, counts, moe_w_gate[0].astype(_BF16),
                    moe_w_up[0].astype(_BF16), moe_w_down[0].astype(_BF16))

    w1, wq, wk, wv = _mla_weights(od_w_in[0], od_w_q_up[0], od_w_kv_up[0])
    tabs = _mla_rope_tables(seq, (MLA_NOPE + MLA_ROPE) ** -0.5 * LOG2E)
    x2, q, k, vt = _mla_proj(dest, x2, ys, attn_norm[1][None, :], w1,
                             od_q_norm[0][None, :], od_kv_norm[0][None, :], wq, wk, wv, tabs, seq)
    o_c = _mla(q.reshape(bsz, seq, -1), k.reshape(bsz, seq, -1), vt)
    wr, br = _router_weights(moe_w_group[1], moe_b_group[1], moe_w_expert[1], moe_b_expert[1])
    x2, rows, cls, rank, counts = _out_proj(x2, [(o_c.reshape(n, -1), od_w_out[0].astype(_BF16))],
                                            ffn_norm[1][None, :], wr, br)
    ys, dest = _moe(rows, cls, rank, counts, moe_w_gate[1].astype(_BF16),
                    moe_w_up[1].astype(_BF16), moe_w_down[1].astype(_BF16))
    out = _final(dest, x2, final_norm[None, :], ys)
    return out.reshape(bsz, seq, d)
```

```python
import functools
import math

import jax
import jax.numpy as jnp
from jax import lax
from jax.experimental import pallas as pl
from jax.experimental.pallas import tpu as pltpu

D_MODEL = 1024
ROPE_THETA = 10000.0
NORM_EPS = 1e-6

HEAD_DIM = 64
MOBA_HEADS = 8
MOBA_BLOCK = 256
MOBA_TOPK = 3
DIFF_HEADS = 4
EVEN_IN = 3072

MLA_HEADS = 16
MLA_Q_RANK = 256
MLA_KV_RANK = 128
MLA_NOPE = 64
MLA_ROPE = 32
MLA_V = 64

N_GROUPS = 4
EXPERTS_PER_GROUP = 8
N_EXPERTS = 32
EXPERT_FF = 256
PAIRS_PER_GROUP = EXPERTS_PER_GROUP * (EXPERTS_PER_GROUP - 1) // 2
N_CLASSES = N_GROUPS * PAIRS_PER_GROUP

LANES = 128
LOG2E = math.log2(math.e)
MASK_BIAS = -1e30
VMEM_LIMIT = 48 * 1024 * 1024

TOKEN_TILE = 512
EXPERT_TILE = 256
ROW_W = D_MODEL + 128
SUBLANES = 8
Q_TILE = 512
Q_GROUP = 4
KV_TILE = 1024
SUB_TILE = 256
assert (Q_TILE * Q_GROUP) % KV_TILE == 0 and KV_TILE % SUB_TILE == 0

_F32 = jnp.float32
_BF16 = jnp.bfloat16


def _dot(a, b):
    return jnp.dot(a, b, preferred_element_type=_F32)


def _rms(x, g):
    return x * lax.rsqrt(jnp.mean(x * x, axis=-1, keepdims=True) + NORM_EPS) * g


def _params(*sem):
    return pltpu.CompilerParams(dimension_semantics=sem, vmem_limit_bytes=VMEM_LIMIT)


def _rope_tables(seq, dim):
    inv = 1.0 / (ROPE_THETA ** (jnp.arange(0, dim, 2, dtype=_F32) / dim))
    ang = jnp.arange(seq, dtype=_F32)[:, None] * inv[None, :]
    return jnp.cos(ang), jnp.sin(ang)


def _even_rope_tables(seq):
    cos, sin = _rope_tables(seq, HEAD_DIM)
    half = HEAD_DIM // 2
    lane = jnp.arange(LANES)
    first = (lane % HEAD_DIM) < half
    c = cos[:, lane % half]
    s = sin[:, lane % half]
    return c, jnp.where(first, -s, 0.0), jnp.where(first, 0.0, s)


def _mla_rope_tables(seq, scale):
    cos, sin = _rope_tables(seq, MLA_ROPE)
    half = MLA_ROPE // 2
    lane = jnp.arange(LANES)
    r = (lane - MLA_NOPE) % half
    in_rope = (lane >= MLA_NOPE) & (lane < MLA_NOPE + MLA_ROPE)
    first = (lane >= MLA_NOPE) & (lane < MLA_NOPE + half)
    second = (lane >= MLA_NOPE + half) & (lane < MLA_NOPE + MLA_ROPE)
    c, s = cos[:, r], sin[:, r]
    cq = jnp.where(lane < MLA_NOPE, 1.0, jnp.where(in_rope, c, 0.0)) * scale
    ck = jnp.where(in_rope, c, 0.0)
    sa = jnp.where(first, -s, 0.0)
    sb = jnp.where(second, s, 0.0)
    return cq, sa * scale, sb * scale, ck, sa, sb


def _rotate(z, c, sa, sb, half):
    return z * c + pltpu.roll(z, LANES - half, 1) * sa + pltpu.roll(z, half, 1) * sb


def _store_transposed(vt_ref, row0, z):
    vt_ref[row0:row0 + LANES, :] = z.T.astype(_BF16)


def _vt_spec(features, tm):
    per_slab = KV_TILE // tm
    return pl.BlockSpec((None, features, tm), lambda i: (i // per_slab, 0, i % per_slab))


def _even_proj_kernel(x_ref, g_ref, w_ref, c_ref, sa_ref, sb_ref, qk_ref, vt_ref):
    h = _rms(x_ref[...], g_ref[...]).astype(_BF16)
    c, sa, sb = c_ref[...], sa_ref[...], sb_ref[...]
    chunk = 512
    blocks = chunk // LANES
    qk_chunk = {0: 0, 1: 1, 3: 2, 4: 3}
    v_chunk = {2: 0, 5: 1}
    for ci in range(EVEN_IN // chunk):
        z = _dot(h, w_ref[:, ci * chunk:(ci + 1) * chunk])
        for b in range(blocks):
            zz = z[:, b * LANES:(b + 1) * LANES]
            if ci in v_chunk:
                _store_transposed(vt_ref, (v_chunk[ci] * blocks + b) * LANES, zz)
                continue
            zz = _rotate(zz, c, sa, sb, HEAD_DIM // 2)
            if ci in (0, 3):
                zz = zz * (HEAD_DIM ** -0.5 * LOG2E)
            col = (qk_chunk[ci] * blocks + b) * LANES
            qk_ref[:, col:col + LANES] = zz.astype(_BF16)


def _even_proj(x2, g, w, tabs, seq):
    n = x2.shape[0]
    tm = TOKEN_TILE
    spb = seq // tm
    tab_spec = pl.BlockSpec((tm, LANES), lambda i: (i % spb, 0))
    qk_w = 4 * 512
    v_w = 2 * 512
    return pl.pallas_call(
        _even_proj_kernel,
        grid=(n // tm,),
        in_specs=[
            pl.BlockSpec((tm, D_MODEL), lambda i: (i, 0)),
            pl.BlockSpec((1, D_MODEL), lambda i: (0, 0)),
            pl.BlockSpec((D_MODEL, EVEN_IN), lambda i: (0, 0)),
            tab_spec, tab_spec, tab_spec,
        ],
        out_specs=[pl.BlockSpec((tm, qk_w), lambda i: (i, 0)), _vt_spec(v_w, tm)],
        out_shape=[jax.ShapeDtypeStruct((n, qk_w), _BF16),
                   jax.ShapeDtypeStruct((n // KV_TILE, v_w, KV_TILE), _BF16)],
        compiler_params=_params("parallel"),
        name="even_proj",
    )(x2, g, w, *tabs)


class _AttnRefs:
    def __init__(self, s_ref, pm_ref, sd_ref, pmd_ref, m_ref, l_ref, acc_ref):
        self.main = [(s_ref, pm_ref, 0), (s_ref, pm_ref, 1)]
        self.diag = [(sd_ref, pmd_ref, 0), (sd_ref, pmd_ref, 1)]
        self.m, self.l, self.acc = m_ref, l_ref, acc_ref


def _attn_scratch(dv, tq):
    scores = pltpu.VMEM((2, 2, KV_TILE, tq), _F32)
    maxima = pltpu.VMEM((2, 2, 8, tq), _F32)
    return [scores, maxima, scores, maxima,
            pltpu.VMEM((2, 1, tq), _F32),
            pltpu.VMEM((2, 1, tq), _F32),
            pltpu.VMEM((2, dv, tq), _F32)]


def _kv_rows(c):
    return pl.ds(pl.multiple_of(c * KV_TILE, KV_TILE), KV_TILE)


def _sub_rows(c, r):
    return pl.ds(pl.multiple_of(c * KV_TILE + r * SUB_TILE, SUB_TILE), SUB_TILE)


def _fold8(x, op):
    return op(x.reshape(x.shape[0] // 8, 8, x.shape[1]), axis=0)


def _stage(refs, value, softmax_of=None, scores_of=None):
    chains = range(2)
    n_sub = KV_TILE // SUB_TILE
    n_in = n_out = 0
    if softmax_of is not None:
        (s_in, pm_in, b_in), c, n_in = softmax_of
        m_new, alpha, acc, lsum = [], [], [], [None, None]
        for t in chains:
            m_prev = refs.m[t]
            m_new.append(jnp.maximum(m_prev, jnp.max(pm_in[b_in, t], axis=0, keepdims=True)))
            alpha.append(jnp.exp2(m_prev - m_new[t]))
            acc.append(alpha[t] * refs.acc[t])
    if scores_of is not None:
        (s_out, pm_out, b_out), c_out, score, mask, n_out = scores_of
        pm = [None, None]
    for r in range(n_sub):
        rows = slice(r * SUB_TILE, (r + 1) * SUB_TILE)
        for t in chains:
            if r < n_out:
                s = score(t, c_out, r)
                if mask is not None:
                    s = mask(s, r)
                s_out[b_out, t, rows, :] = s
                part = _fold8(s, jnp.max)
                pm[t] = part if pm[t] is None else jnp.maximum(pm[t], part)
            if r < n_in:
                p = jnp.exp2(s_in[b_in, t, rows, :] - m_new[t])
                part = _fold8(p, jnp.sum)
                lsum[t] = part if lsum[t] is None else lsum[t] + part
                acc[t] = acc[t] + _dot(value(t, c, r), p.astype(_BF16))
    for t in chains:
        if scores_of is not None:
            pm_out[b_out, t] = pm[t]
        if softmax_of is not None:
            refs.m[t] = m_new[t]
            refs.l[t] = alpha[t] * refs.l[t] + jnp.sum(lsum[t], axis=0, keepdims=True)
            refs.acc[t] = acc[t]


def _sweep(refs, first_qi, make_score, value, finalize):
    tq = refs.m.shape[-1]
    n_sub = KV_TILE // SUB_TILE
    m0, m1 = refs.main
    delta = (lax.broadcasted_iota(jnp.int32, (SUB_TILE, tq), 0)
             - lax.broadcasted_iota(jnp.int32, (SUB_TILE, tq), 1))

    def plan(j):
        last = ((first_qi + j) * tq) // KV_TILE
        start = (j * tq) % KV_TILE
        n_diag = min(n_sub, (start + tq - 1) // SUB_TILE + 1)

        def causal(s, r):
            if (r + 1) * SUB_TILE - 1 <= start:
                return s
            return jnp.where(delta <= start - r * SUB_TILE, s, -jnp.inf)

        return last, n_diag, (refs.diag[j % 2], last, make_score(j), causal, n_diag)

    plans = [plan(j) for j in range(Q_GROUP)]
    _stage(refs, value, scores_of=plans[0][2])

    for j, (last, n_diag, (diag, _, score, _, _)) in enumerate(plans):
        refs.m[...] = jnp.full(refs.m.shape, -jnp.inf, _F32)
        refs.l[...] = jnp.zeros(refs.l.shape, _F32)
        refs.acc[...] = jnp.zeros(refs.acc.shape, _F32)
        odd = last % 2

        def below(buf, tile):
            return (buf, tile, score, None, n_sub)

        @pl.when(odd == 1)
        def _():
            _stage(refs, value, softmax_of=(diag, last, n_diag), scores_of=below(m1, 0))

        @pl.when(jnp.logical_and(odd == 0, last >= 2))
        def _():
            _stage(refs, value, softmax_of=(diag, last, n_diag), scores_of=below(m0, 0))
            _stage(refs, value, softmax_of=(m0, 0, n_sub), scores_of=below(m1, 1))

        base = 1 - odd

        def pair(k, carry):
            c = base + 2 * k
            _stage(refs, value, softmax_of=(m1, c, n_sub), scores_of=below(m0, c + 1))
            _stage(refs, value, softmax_of=(m0, c + 1, n_sub), scores_of=below(m1, c + 2))
            return carry

        lax.fori_loop(0, lax.shift_right_arithmetic(last - 1 - base, 1), pair, 0)

        nxt = plans[j + 1][2] if j + 1 < Q_GROUP else None

        @pl.when(last == 0)
        def _():
            _stage(refs, value, softmax_of=(diag, last, n_diag), scores_of=nxt)

        @pl.when(last > 0)
        def _():
            _stage(refs, value, softmax_of=(m1, last - 1, n_sub), scores_of=nxt)

        finalize(j)


def _transposed(q):
    return q.astype(_F32).T


def _moba_kernel(q_ref, k_ref, vt_ref, o_ref, km_ref, ka_ref, *scratch, nb, topk):
    step = pl.program_id(2)
    tq = Q_TILE
    half = HEAD_DIM // 2

    @pl.when(step == 0)
    def _():
        kf = k_ref[...].astype(_F32)
        means = jnp.sum(kf.reshape(nb, MOBA_BLOCK, LANES), axis=1) * (1.0 / MOBA_BLOCK)
        lane_b = lax.broadcasted_iota(jnp.int32, (nb, LANES), 1)
        km_ref[...] = jnp.zeros(km_ref.shape, _F32)
        km_ref[0:nb, :] = jnp.where(lane_b < HEAD_DIM, 0.0, means)
        km_ref[HEAD_DIM:HEAD_DIM + nb, :] = jnp.where(lane_b < HEAD_DIM, means, 0.0)

        lane = lax.broadcasted_iota(jnp.int32, (KV_TILE, LANES), 1)
        blk = lax.broadcasted_iota(jnp.int32, (KV_TILE, LANES), 0) // MOBA_BLOCK

        def build(c, carry):
            rows = _kv_rows(c)
            k = k_ref[rows, :].astype(_F32)
            b = blk + c * (KV_TILE // MOBA_BLOCK)
            ka_ref[0, rows, :] = jnp.where(lane < HEAD_DIM, k,
                                           jnp.where(lane == b + HEAD_DIM, 1.0, 0.0)).astype(_BF16)
            ka_ref[1, rows, :] = jnp.where(lane < HEAD_DIM,
                                           jnp.where(lane == b, 1.0, 0.0), k).astype(_BF16)
            return carry

        lax.fori_loop(0, k_ref.shape[0] // KV_TILE, build, 0)

    row = lax.broadcasted_iota(jnp.int32, (half, tq), 0).astype(_F32)
    z_half = jnp.zeros((half, tq), _F32)

    def make_score(j):
        qt = _transposed(q_ref[j * tq:(j + 1) * tq, :])
        gate = _dot(km_ref[...].astype(_BF16), qt.astype(_BF16))
        qpos = (step * Q_GROUP + j) * tq + lax.broadcasted_iota(jnp.int32, (half, tq), 1)
        own = (qpos // MOBA_BLOCK).astype(_F32)

        def block_bias(base):
            g = jnp.where(row < own, gate[base:base + half], -jnp.inf)
            bias = jnp.where(row == own, 0.0, MASK_BIAS)
            for _ in range(topk):
                mx = jnp.max(g, axis=0, keepdims=True)
                hit = (g == mx) & (mx > -jnp.inf)
                idx = jnp.min(jnp.where(hit, row, 4.0 * LANES), axis=0, keepdims=True)
                pick = row == idx
                bias = jnp.where(pick, 0.0, bias)
                g = jnp.where(pick, -jnp.inf, g)
            return bias

        qs = (jnp.concatenate([qt[:HEAD_DIM], block_bias(HEAD_DIM), z_half], axis=0).astype(_BF16),
              jnp.concatenate([block_bias(0), z_half, qt[HEAD_DIM:]], axis=0).astype(_BF16))
        return lambda t, c, r: _dot(ka_ref[t, _sub_rows(c, r), :], qs[t])

    def value(t, c, r):
        return vt_ref[c, t * HEAD_DIM:(t + 1) * HEAD_DIM, r * SUB_TILE:(r + 1) * SUB_TILE]

    refs = _AttnRefs(*scratch)

    def finalize(j):
        ot = jnp.concatenate([refs.acc[0] / refs.l[0], refs.acc[1] / refs.l[1]], axis=0)
        o_ref[j * tq:(j + 1) * tq, :] = ot.T.astype(o_ref.dtype)

    _sweep(refs, step * Q_GROUP, make_score, value, finalize)


def _moba(qk3, vt):
    bsz, seq, _ = qk3.shape
    tq = Q_TILE * Q_GROUP
    assert Q_TILE % MOBA_BLOCK == 0 and seq % KV_TILE == 0 and seq % tq == 0
    nb = seq // MOBA_BLOCK
    assert nb <= HEAD_DIM // 2
    topk = max(1, min(MOBA_TOPK, nb - 1))
    pairs = MOBA_HEADS // 2
    return pl.pallas_call(
        functools.partial(_moba_kernel, nb=nb, topk=topk),
        grid=(bsz, pairs, seq // tq),
        in_specs=[
            pl.BlockSpec((None, tq, LANES), lambda b, p, i: (b, i, p)),
            pl.BlockSpec((None, seq, LANES), lambda b, p, i: (b, 0, pairs + p)),
            pl.BlockSpec((seq // KV_TILE, LANES, KV_TILE), lambda b, p, i: (b, p, 0)),
        ],
        out_specs=pl.BlockSpec((None, tq, LANES), lambda b, p, i: (b, i, p)),
        out_shape=jax.ShapeDtypeStruct((bsz, seq, pairs * LANES), _BF16),
        scratch_shapes=[pltpu.VMEM((LANES, LANES), _F32), pltpu.VMEM((2, seq, LANES), _BF16)]
                       + _attn_scratch(HEAD_DIM, Q_TILE),
        compiler_params=_params("parallel", "parallel", "arbitrary"),
        name="moba_attn",
    )(qk3, qk3, vt)


def _diff_kernel(q_ref, k_ref, vt_ref, lam_ref, g_ref, o_ref, *scratch, lam_init):
    step = pl.program_id(2)
    tq = Q_TILE
    z_head = jnp.zeros((HEAD_DIM, tq), _F32)

    def make_score(j):
        qt = _transposed(q_ref[j * tq:(j + 1) * tq, :])
        qd = (jnp.concatenate([qt[:HEAD_DIM], z_head], axis=0).astype(_BF16),
              jnp.concatenate([z_head, qt[HEAD_DIM:]], axis=0).astype(_BF16))
        return lambda t, c, r: _dot(k_ref[_sub_rows(c, r), :], qd[t])

    def value(t, c, r):
        return vt_ref[c, :, r * SUB_TILE:(r + 1) * SUB_TILE]

    lv = lam_ref[...]
    lam = (jnp.exp(jnp.sum(lv[0:1] * lv[1:2], axis=1, keepdims=True))
           - jnp.exp(jnp.sum(lv[2:3] * lv[3:4], axis=1, keepdims=True)) + lam_init)
    refs = _AttnRefs(*scratch)

    def finalize(j):
        ot = refs.acc[0] / refs.l[0] - lam * (refs.acc[1] / refs.l[1])
        o_ref[j * tq:(j + 1) * tq, :] = (_rms(ot.T, g_ref[...]) * (1.0 - lam_init)).astype(o_ref.dtype)

    _sweep(refs, step * Q_GROUP, make_score, value, finalize)


def _diff(qk3, vt, lam_rows, subln, lam_init):
    bsz, seq, _ = qk3.shape
    tq = Q_TILE * Q_GROUP
    base = 2 * MOBA_HEADS * HEAD_DIM // LANES
    vbase = MOBA_HEADS * HEAD_DIM // LANES
    return pl.pallas_call(
        functools.partial(_diff_kernel, lam_init=lam_init),
        grid=(bsz, DIFF_HEADS, seq // tq),
        in_specs=[
            pl.BlockSpec((None, tq, LANES), lambda b, h, i: (b, i, base + h)),
            pl.BlockSpec((None, seq, LANES), lambda b, h, i: (b, 0, base + DIFF_HEADS + h)),
            pl.BlockSpec((seq // KV_TILE, LANES, KV_TILE), lambda b, h, i: (b, vbase + h, 0)),
            pl.BlockSpec((4, HEAD_DIM), lambda b, h, i: (0, 0)),
            pl.BlockSpec((1, LANES), lambda b, h, i: (0, 0)),
        ],
        out_specs=pl.BlockSpec((None, tq, LANES), lambda b, h, i: (b, i, h)),
        out_shape=jax.ShapeDtypeStruct((bsz, seq, DIFF_HEADS * LANES), _BF16),
        scratch_shapes=_attn_scratch(LANES, Q_TILE),
        compiler_params=_params("parallel", "parallel", "arbitrary"),
        name="diff_attn",
    )(qk3, qk3, vt, lam_rows, subln)


def _mla_proj_kernel(dest_ref, next_ref, x_ref, ys_ref,
                     g_ref, w1_ref, qg_ref, kg_ref, wq_ref, wk_ref, wv_ref,
                     cq_ref, saq_ref, sbq_ref, ck_ref, sak_ref, sbk_ref,
                     x_out, q_out, k_out, vt_out, buf_ref, sems):
    x = x_ref[...] + _gather_rows(dest_ref, next_ref, ys_ref, buf_ref, sems)
    x_out[...] = x
    h = _rms(x, g_ref[...]).astype(_BF16)
    z = _dot(h, w1_ref[...])
    cq = _rms(z[:, :MLA_Q_RANK], qg_ref[...]).astype(_BF16)
    ckv = _rms(z[:, MLA_Q_RANK:MLA_Q_RANK + MLA_KV_RANK], kg_ref[...]).astype(_BF16)
    kr = z[:, MLA_Q_RANK + MLA_KV_RANK:]
    half = MLA_ROPE // 2
    kr = _rotate(kr, ck_ref[...], sak_ref[...], sbk_ref[...], half)
    qf = _dot(cq, wq_ref[...])
    kf = _dot(ckv, wk_ref[...])
    cqt, saq, sbq = cq_ref[...], saq_ref[...], sbq_ref[...]
    for hd in range(MLA_HEADS):
        sl = slice(hd * LANES, (hd + 1) * LANES)
        q_out[:, sl] = _rotate(qf[:, sl], cqt, saq, sbq, half).astype(_BF16)
        k_out[:, sl] = (kf[:, sl] + kr).astype(_BF16)
    vf = _dot(ckv, wv_ref[...])
    for b in range(MLA_HEADS * MLA_V // LANES):
        _store_transposed(vt_out, b * LANES, vf[:, b * LANES:(b + 1) * LANES])


def _mla_proj(dest, x2, ys, g, w1, qg, kg, wq, wk, wv, tabs, seq):
    n = x2.shape[0]
    tm = TOKEN_TILE
    spb = seq // tm
    hw = MLA_HEADS * LANES
    vw = MLA_HEADS * MLA_V
    tab_spec = pl.BlockSpec((tm, LANES), lambda i: (i % spb, 0))

    def full(a):
        return pl.BlockSpec(a.shape, lambda i: (0,) * a.ndim)

    return pl.pallas_call(
        _mla_proj_kernel,
        grid=(n // tm,),
        in_specs=[_index_spec(tm), _index_spec(tm, n // tm - 1),
                  pl.BlockSpec((tm, D_MODEL), lambda i: (i, 0)),
                  pl.BlockSpec(memory_space=pl.ANY),
                  full(g), full(w1), full(qg), full(kg), full(wq), full(wk), full(wv)]
                 + [tab_spec] * 6,
        out_specs=[pl.BlockSpec((tm, D_MODEL), lambda i: (i, 0)),
                   pl.BlockSpec((tm, hw), lambda i: (i, 0)),
                   pl.BlockSpec((tm, hw), lambda i: (i, 0)),
                   _vt_spec(vw, tm)],
        out_shape=[jax.ShapeDtypeStruct((n, D_MODEL), _F32),
                   jax.ShapeDtypeStruct((n, hw), _BF16),
                   jax.ShapeDtypeStruct((n, hw), _BF16),
                   jax.ShapeDtypeStruct((n // KV_TILE, vw, KV_TILE), _BF16)],
        scratch_shapes=_gather_scratch(tm),
        compiler_params=_params("arbitrary"),
        name="mla_proj",
    )(dest, dest, x2, ys, g, w1, qg, kg, wq, wk, wv, *tabs)


def _mla_kernel(q_ref, k_ref, vt_ref, o_ref, *scratch):
    step = pl.program_id(2)
    tq = Q_TILE

    def make_score(j):
        q = q_ref[j * tq:(j + 1) * tq, :]
        qd = tuple(_transposed(q[:, t * LANES:(t + 1) * LANES]).astype(_BF16) for t in range(2))
        return lambda t, c, r: _dot(k_ref[_sub_rows(c, r), t * LANES:(t + 1) * LANES], qd[t])

    def value(t, c, r):
        return vt_ref[c, t * MLA_V:(t + 1) * MLA_V, r * SUB_TILE:(r + 1) * SUB_TILE]

    refs = _AttnRefs(*scratch)

    def finalize(j):
        ot = jnp.concatenate([refs.acc[0] / refs.l[0], refs.acc[1] / refs.l[1]], axis=0)
        o_ref[j * tq:(j + 1) * tq, :] = ot.T.astype(o_ref.dtype)

    _sweep(refs, step * Q_GROUP, make_score, value, finalize)


def _mla(q3, k3, vt):
    bsz, seq, _ = q3.shape
    tq = Q_TILE * Q_GROUP
    pairs = MLA_HEADS // 2
    return pl.pallas_call(
        _mla_kernel,
        grid=(bsz, pairs, seq // tq),
        in_specs=[
            pl.BlockSpec((None, tq, 2 * LANES), lambda b, p, i: (b, i, p)),
            pl.BlockSpec((None, seq, 2 * LANES), lambda b, p, i: (b, 0, p)),
            pl.BlockSpec((seq // KV_TILE, LANES, KV_TILE), lambda b, p, i: (b, p, 0)),
        ],
        out_specs=pl.BlockSpec((None, tq, LANES), lambda b, p, i: (b, i, p)),
        out_shape=jax.ShapeDtypeStruct((bsz, seq, pairs * LANES), _BF16),
        scratch_shapes=_attn_scratch(MLA_V, Q_TILE),
        compiler_params=_params("parallel", "parallel", "arbitrary"),
        name="mla_attn",
    )(q3, k3, vt)


def _route(logits):
    lane = lax.broadcasted_iota(jnp.int32, logits.shape, 1).astype(_F32)
    big = 4.0 * LANES
    is_group = (lane >= N_EXPERTS) & (lane < N_EXPERTS + N_GROUPS)
    gl = jnp.where(is_group, logits, -jnp.inf)
    gmax = jnp.max(gl, axis=1, keepdims=True)
    grp = jnp.min(jnp.where(gl == gmax, lane, big), axis=1, keepdims=True) - N_EXPERTS
    p_group = 1.0 / jnp.sum(jnp.exp(gl - gmax), axis=1, keepdims=True)
    in_group = (lane >= grp * EXPERTS_PER_GROUP) & (lane < (grp + 1) * EXPERTS_PER_GROUP)
    el = jnp.where(in_group, logits, -jnp.inf)
    e1 = jnp.max(el, axis=1, keepdims=True)
    i1 = jnp.min(jnp.where(el == e1, lane, big), axis=1, keepdims=True)
    el2 = jnp.where(lane == i1, -jnp.inf, el)
    e2 = jnp.max(el2, axis=1, keepdims=True)
    i2 = jnp.min(jnp.where(el2 == e2, lane, big), axis=1, keepdims=True)
    r = jnp.exp(e2 - e1)
    w1 = p_group / (1.0 + r)
    w2 = w1 * r
    first = i1 < i2
    a = jnp.minimum(i1, i2) - grp * EXPERTS_PER_GROUP
    b = jnp.maximum(i1, i2) - grp * EXPERTS_PER_GROUP
    cls = grp * PAIRS_PER_GROUP + a * (2 * EXPERTS_PER_GROUP - 1 - a) * 0.5 + (b - a - 1.0)
    return jnp.where(first, w1, w2), jnp.where(first, w2, w1), cls


def _lane_dense(col, eye):
    t = col.shape[0]
    return jnp.sum(jnp.where(eye, col, 0.0).reshape(t // LANES, LANES, LANES), axis=1)


def _out_proj_kernel(*refs, n_mix):
    x_ref = refs[0]
    mix = refs[1:1 + 2 * n_mix]
    g_ref, wr_ref, br_ref = refs[1 + 2 * n_mix:4 + 2 * n_mix]
    x_out, row_out, cls_out, rank_out, count_out, count_ref = refs[4 + 2 * n_mix:]

    @pl.when(pl.program_id(0) == 0)
    def _():
        count_ref[...] = jnp.zeros(count_ref.shape, _F32)

    x = x_ref[...]
    for i in range(n_mix):
        x = x + _dot(mix[2 * i][...], mix[2 * i + 1][...])
    x_out[...] = x
    t = _rms(x, g_ref[...])
    w_lo, w_hi, cls = _route(_dot(t.astype(_BF16), wr_ref[...]) + br_ref[...])

    tm = x.shape[0]
    lane = lax.broadcasted_iota(jnp.int32, (tm, LANES), 1)
    row_out[:, :D_MODEL] = t
    row_out[:, D_MODEL:] = jnp.where(lane == 0, w_lo, 0.0) + jnp.where(lane == 1, w_hi, 0.0)

    onehot = jnp.where(lane.astype(_F32) == cls, 1.0, 0.0)
    earlier = (lax.broadcasted_iota(jnp.int32, (tm, tm), 1)
               < lax.broadcasted_iota(jnp.int32, (tm, tm), 0))
    before = _dot(jnp.where(earlier, 1.0, 0.0).astype(_BF16), onehot.astype(_BF16))
    rank = jnp.sum(onehot * (before + count_ref[...]), axis=1, keepdims=True)
    count_ref[...] += jnp.sum(onehot, axis=0, keepdims=True)
    count_out[...] = count_ref[...].astype(jnp.int32)
    eye = lax.broadcasted_iota(jnp.int32, (tm, LANES), 0) % LANES == lane
    cls_out[...] = _lane_dense(cls, eye).astype(jnp.int32)
    rank_out[...] = _lane_dense(rank, eye).astype(jnp.int32)


def _out_proj(x2, mixes, g, wr, br):
    n = x2.shape[0]
    tm = TOKEN_TILE
    in_specs = [pl.BlockSpec((tm, D_MODEL), lambda i: (i, 0))]
    args = [x2]
    for o, w in mixes:
        in_specs.append(pl.BlockSpec((tm, o.shape[1]), lambda i: (i, 0)))
        in_specs.append(pl.BlockSpec(w.shape, lambda i: (0, 0)))
        args += [o, w]
    for a in (g, wr, br):
        in_specs.append(pl.BlockSpec(a.shape, lambda i: (0, 0)))
        args.append(a)
    idx_spec = pl.BlockSpec((None, tm // LANES, LANES), lambda i: (i, 0, 0))
    idx_shape = jax.ShapeDtypeStruct((n // tm, tm // LANES, LANES), jnp.int32)
    return pl.pallas_call(
        functools.partial(_out_proj_kernel, n_mix=len(mixes)),
        grid=(n // tm,),
        in_specs=in_specs,
        out_specs=[pl.BlockSpec((tm, D_MODEL), lambda i: (i, 0)),
                   pl.BlockSpec((tm, ROW_W), lambda i: (i, 0)),
                   idx_spec, idx_spec,
                   pl.BlockSpec((1, LANES), lambda i: (0, 0))],
        out_shape=[jax.ShapeDtypeStruct((n, D_MODEL), _F32),
                   jax.ShapeDtypeStruct((n, ROW_W), _F32),
                   idx_shape, idx_shape,
                   jax.ShapeDtypeStruct((1, LANES), jnp.int32)],
        scratch_shapes=[pltpu.VMEM((1, LANES), _F32)],
        compiler_params=_params("arbitrary"),
        name="out_proj_route",
    )(*args)


def _class_experts():
    pairs = [(a, b) for a in range(EXPERTS_PER_GROUP) for b in range(a + 1, EXPERTS_PER_GROUP)]
    lo = [g * EXPERTS_PER_GROUP + a for g in range(N_GROUPS) for a, _ in pairs]
    hi = [g * EXPERTS_PER_GROUP + b for g in range(N_GROUPS) for _, b in pairs]
    return jnp.asarray(lo, jnp.int32), jnp.asarray(hi, jnp.int32)


def _tile_plan(counts, max_tiles):
    tiles = (counts + EXPERT_TILE - 1) // EXPERT_TILE
    ends = jnp.cumsum(tiles)
    offsets = (ends - tiles) * EXPERT_TILE
    tile = jnp.arange(max_tiles, dtype=jnp.int32)
    tile_cls = jnp.sum((ends[None, :] <= tile[:, None]).astype(jnp.int32), axis=1)
    tile_cls = jnp.minimum(tile_cls, N_CLASSES - 1)
    lo, hi = _class_experts()
    return offsets.astype(jnp.int32), lo[tile_cls], hi[tile_cls], ends[-1:].astype(jnp.int32)


def _sorted_rows_kernel(cls_ref, rank_ref, off_ref, dest_ref):
    table = jnp.broadcast_to(off_ref[...], cls_ref.shape)
    dest_ref[...] = jnp.take_along_axis(table, cls_ref[...], axis=1) + rank_ref[...]


def _sorted_rows(cls, rank, offsets):
    spec = pl.BlockSpec((None,) + cls.shape[1:], lambda i: (i, 0, 0))
    return pl.pallas_call(
        _sorted_rows_kernel,
        grid=(cls.shape[0],),
        in_specs=[spec, spec, pl.BlockSpec((1, LANES), lambda i: (0, 0))],
        out_specs=spec,
        out_shape=jax.ShapeDtypeStruct(cls.shape, jnp.int32),
        compiler_params=_params("parallel"),
        name="moe_sorted_rows",
    )(cls, rank, offsets[None, :])


def _row_copies(dest_ref, copy, method):
    tiles_per_row = LANES // SUBLANES
    for a in range(dest_ref.shape[0]):
        def body(g, carry, a=a):
            for j in range(SUBLANES):
                d = dest_ref[a, g * SUBLANES + j]
                getattr(copy(a * tiles_per_row + g, j, d), method)()
            return carry

        lax.fori_loop(0, tiles_per_row, body, 0)


def _dispatch_kernel(dest_ref, row_ref, xs_in_ref, xs_ref, sem):
    del xs_in_ref

    def copy(tile, j, d):
        return pltpu.make_async_copy(row_ref.at[tile, pl.ds(j, 1), :],
                                     xs_ref.at[pl.ds(d, 1), :], sem)

    _row_copies(dest_ref, copy, "start")
    _row_copies(dest_ref, copy, "wait")


def _index_spec(tm, last_step=None):
    if last_step is None:
        index = lambda i: (i, 0, 0)
    else:
        index = lambda i: (jnp.minimum(i + 1, last_step), 0, 0)
    return pl.BlockSpec((None, tm // LANES, LANES), index, memory_space=pltpu.SMEM)


def _dispatch(dest, rows, max_tiles):
    n = rows.shape[0]
    tm = TOKEN_TILE
    xs0 = jnp.zeros((max_tiles * EXPERT_TILE, ROW_W), _F32)
    return pl.pallas_call(
        _dispatch_kernel,
        grid=(n // tm,),
        in_specs=[_index_spec(tm),
                  pl.BlockSpec((tm // SUBLANES, SUBLANES, ROW_W), lambda i: (i, 0, 0)),
                  pl.BlockSpec(memory_space=pl.ANY)],
        out_specs=pl.BlockSpec(memory_space=pl.ANY),
        out_shape=jax.ShapeDtypeStruct(xs0.shape, _F32),
        scratch_shapes=[pltpu.SemaphoreType.DMA(())],
        input_output_aliases={2: 0},
        compiler_params=_params("arbitrary"),
        name="moe_dispatch",
    )(dest, rows.reshape(n // SUBLANES, SUBLANES, ROW_W), xs0)


def _expert_kernel(lo_ref, hi_ref, used_ref, xs_ref, wg0, wu0, wd0, wg1, wu1, wd1, ys_ref):
    del lo_ref, hi_ref
    i = pl.program_id(0)

    @pl.when(i < used_ref[0])
    def _():
        xs = xs_ref[...]
        x = xs[:, :D_MODEL].astype(_BF16)
        gates = xs[:, D_MODEL:]
        ups = [(_dot(x, wg[...]), _dot(x, wu[...])) for wg, wu in ((wg0, wu0), (wg1, wu1))]
        acts = [(hg * (1.0 / (1.0 + jnp.exp(-hg))) * hu * gates[:, k:k + 1]).astype(_BF16)
                for k, (hg, hu) in enumerate(ups)]
        ys_ref[...] = _dot(acts[0], wd0[...]) + _dot(acts[1], wd1[...])

    @pl.when(i >= used_ref[0])
    def _():
        ys_ref[...] = jnp.zeros(ys_ref.shape, _F32)


def _experts(xs, tile_lo, tile_hi, used, wg, wu, wd):
    max_tiles = xs.shape[0] // EXPERT_TILE
    up = (None, D_MODEL, EXPERT_FF)
    down = (None, EXPERT_FF, D_MODEL)
    by_lo = lambda i, lo, hi, used: (lo[i], 0, 0)
    by_hi = lambda i, lo, hi, used: (hi[i], 0, 0)
    grid_spec = pltpu.PrefetchScalarGridSpec(
        num_scalar_prefetch=3,
        grid=(max_tiles,),
        in_specs=[pl.BlockSpec((EXPERT_TILE, ROW_W),
                               lambda i, lo, hi, used: (jnp.minimum(i, used[0] - 1), 0)),
                  pl.BlockSpec(up, by_lo), pl.BlockSpec(up, by_lo), pl.BlockSpec(down, by_lo),
                  pl.BlockSpec(up, by_hi), pl.BlockSpec(up, by_hi), pl.BlockSpec(down, by_hi)],
        out_specs=pl.BlockSpec((EXPERT_TILE, D_MODEL), lambda i, lo, hi, used: (i, 0)),
    )
    return pl.pallas_call(
        _expert_kernel,
        grid_spec=grid_spec,
        out_shape=jax.ShapeDtypeStruct((xs.shape[0], D_MODEL), _F32),
        compiler_params=_params("arbitrary"),
        name="moe_experts",
    )(tile_lo, tile_hi, used, xs, wg, wu, wd, wg, wu, wd)


def _gather_rows(dest_ref, next_ref, ys_ref, buf_ref, sems):
    step = pl.program_id(0)
    slot = step % 2

    def copies(half):
        def copy(tile, j, d):
            return pltpu.make_async_copy(ys_ref.at[pl.ds(d, 1), :],
                                         buf_ref.at[half, tile, pl.ds(j, 1), :], sems.at[half])
        return copy

    @pl.when(step == 0)
    def _():
        _row_copies(dest_ref, copies(0), "start")

    @pl.when(step + 1 < pl.num_programs(0))
    def _():
        _row_copies(next_ref, copies(1 - slot), "start")

    _row_copies(dest_ref, copies(slot), "wait")
    return buf_ref[slot].reshape(buf_ref.shape[1] * SUBLANES, buf_ref.shape[3])


def _gather_scratch(tm):
    return [pltpu.VMEM((2, tm // SUBLANES, SUBLANES, D_MODEL), _F32),
            pltpu.SemaphoreType.DMA((2,))]


def _final_kernel(dest_ref, next_ref, x_ref, g_ref, ys_ref, o_ref, buf_ref, sems):
    rows = _gather_rows(dest_ref, next_ref, ys_ref, buf_ref, sems)
    o_ref[...] = _rms(x_ref[...] + rows, g_ref[...])


def _final(dest, x2, g, ys):
    n = x2.shape[0]
    tm = TOKEN_TILE
    return pl.pallas_call(
        _final_kernel,
        grid=(n // tm,),
        in_specs=[_index_spec(tm), _index_spec(tm, n // tm - 1),
                  pl.BlockSpec((tm, D_MODEL), lambda i: (i, 0)),
                  pl.BlockSpec((1, D_MODEL), lambda i: (0, 0)),
                  pl.BlockSpec(memory_space=pl.ANY)],
        out_specs=pl.BlockSpec((tm, D_MODEL), lambda i: (i, 0)),
        out_shape=jax.ShapeDtypeStruct((n, D_MODEL), _F32),
        scratch_shapes=_gather_scratch(tm),
        compiler_params=_params("arbitrary"),
        name="moe_combine_norm",
    )(dest, dest, x2, g, ys)


def _moe(rows, cls, rank, counts, wg, wu, wd):
    n = rows.shape[0]
    max_tiles = n // EXPERT_TILE + N_CLASSES
    offsets, tile_lo, tile_hi, used = _tile_plan(counts[0], max_tiles)
    dest = _sorted_rows(cls, rank, offsets)
    xs = _dispatch(dest, rows, max_tiles)
    return _experts(xs, tile_lo, tile_hi, used, wg, wu, wd), dest


def _router_weights(w_group, b_group, w_expert, b_expert):
    pad = LANES - N_EXPERTS - N_GROUPS
    wr = jnp.concatenate([w_expert, w_group, jnp.zeros((D_MODEL, pad), _F32)], axis=1)
    br = jnp.concatenate([b_expert, b_group, jnp.zeros((pad,), _F32)])[None, :]
    return wr.astype(_BF16), br


def _mla_weights(w_in, w_q_up, w_kv_up):
    lat = MLA_Q_RANK + MLA_KV_RANK
    w1 = jnp.concatenate([w_in[:, :lat], jnp.zeros((D_MODEL, MLA_NOPE), _F32), w_in[:, lat:],
                          jnp.zeros((D_MODEL, LANES - MLA_NOPE - MLA_ROPE), _F32)], axis=1)
    qd = MLA_NOPE + MLA_ROPE
    wq = jnp.pad(w_q_up.reshape(MLA_Q_RANK, MLA_HEADS, qd), ((0, 0), (0, 0), (0, LANES - qd)))
    kv = w_kv_up.reshape(MLA_KV_RANK, MLA_HEADS, MLA_NOPE + MLA_V)
    wk = jnp.pad(kv[:, :, :MLA_NOPE], ((0, 0), (0, 0), (0, LANES - MLA_NOPE)))
    wv = kv[:, :, MLA_NOPE:]
    return (w1.astype(_BF16), wq.reshape(MLA_Q_RANK, -1).astype(_BF16),
            wk.reshape(MLA_KV_RANK, -1).astype(_BF16), wv.reshape(MLA_KV_RANK, -1).astype(_BF16))


def kernel(x, attn_norm, ev_w_in, ev_lambda_q1, ev_lambda_k1, ev_lambda_q2, ev_lambda_k2, ev_subln, ev_w_out, od_w_in, od_q_norm, od_kv_norm, od_w_q_up, od_w_kv_up, od_w_out, ffn_norm, moe_w_group, moe_b_group, moe_w_expert, moe_b_expert, moe_w_gate, moe_w_up, moe_w_down, final_norm):
    bsz, seq, d = x.shape
    n = bsz * seq
    x2 = x.reshape(n, d)

    lam_init = 0.8 - 0.6 * math.exp(-0.3 * 0)
    qk, vt = _even_proj(x2, attn_norm[0][None, :], ev_w_in[0].astype(_BF16),
                        _even_rope_tables(seq), seq)
    qk3 = qk.reshape(bsz, seq, -1)
    o_a = _moba(qk3, vt).reshape(n, -1)
    lam_rows = jnp.stack([ev_lambda_q1[0], ev_lambda_k1[0], ev_lambda_q2[0], ev_lambda_k2[0]])
    o_b = _diff(qk3, vt, lam_rows, ev_subln[0][None, :], lam_init).reshape(n, -1)
    w_out = ev_w_out[0].astype(_BF16)
    wa, wb = w_out[:o_a.shape[1]], w_out[o_a.shape[1]:]
    wr, br = _router_weights(moe_w_group[0], moe_b_group[0], moe_w_expert[0], moe_b_expert[0])
    x2, rows, cls, rank, counts = _out_proj(x2, [(o_a, wa), (o_b, wb)], ffn_norm[0][None, :],
                                            wr, br)
    ys, dest = _moe(rows, cls, rank, counts, moe_w_gate[0].astype(_BF16),
                    moe_w_up[0].astype(_BF16), moe_w_down[0].astype(_BF16))

    w1, wq, wk, wv = _mla_weights(od_w_in[0], od_w_q_up[0], od_w_kv_up[0])
    tabs = _mla_rope_tables(seq, (MLA_NOPE + MLA_ROPE) ** -0.5 * LOG2E)
    x2, q, k, vt = _mla_proj(dest, x2, ys, attn_norm[1][None, :], w1,
                             od_q_norm[0][None, :], od_kv_norm[0][None, :], wq, wk, wv, tabs, seq)
    o_c = _mla(q.reshape(bsz, seq, -1), k.reshape(bsz, seq, -1), vt)
    wr, br = _router_weights(moe_w_group[1], moe_b_group[1], moe_w_expert[1], moe_b_expert[1])
    x2, rows, cls, rank, counts = _out_proj(x2, [(o_c.reshape(n, -1), od_w_out[0].astype(_BF16))],
                                            ffn_norm[1][None, :], wr, br)
    ys, dest = _moe(rows, cls, rank, counts, moe_w_gate[1].astype(_BF16),
                    moe_w_up[1].astype(_BF16), moe_w_down[1].astype(_BF16))
    out = _final(dest, x2, final_norm[None, :], ys)
    return out.reshape(bsz, seq, d)
```

```python
import functools
import math

import jax
import jax.numpy as jnp
from jax import lax
from jax.experimental import pallas as pl
from jax.experimental.pallas import tpu as pltpu

D_MODEL = 1024
ROPE_THETA = 10000.0
NORM_EPS = 1e-6

HEAD_DIM = 64
MOBA_HEADS = 8
MOBA_BLOCK = 256
MOBA_TOPK = 3
DIFF_HEADS = 4
EVEN_IN = 3072

MLA_HEADS = 16
MLA_Q_RANK = 256
MLA_KV_RANK = 128
MLA_NOPE = 64
MLA_ROPE = 32
MLA_V = 64

N_GROUPS = 4
EXPERTS_PER_GROUP = 8
N_EXPERTS = 32
EXPERT_FF = 256
PAIRS_PER_GROUP = EXPERTS_PER_GROUP * (EXPERTS_PER_GROUP - 1) // 2
N_CLASSES = N_GROUPS * PAIRS_PER_GROUP

LANES = 128
LOG2E = math.log2(math.e)
MASK_BIAS = -1e30
VMEM_LIMIT = 48 * 1024 * 1024

TOKEN_TILE = 512
EXPERT_TILE = 256
ROW_W = D_MODEL // 2 + 128
SUBLANES = 8
Q_TILE = 512
Q_GROUP = 4
KV_TILE = 1024
SUB_TILE = 256
assert (Q_TILE * Q_GROUP) % KV_TILE == 0 and KV_TILE % SUB_TILE == 0

_F32 = jnp.float32
_BF16 = jnp.bfloat16


def _dot(a, b):
    return jnp.dot(a, b, preferred_element_type=_F32)


def _rms(x, g):
    return x * lax.rsqrt(jnp.mean(x * x, axis=-1, keepdims=True) + NORM_EPS) * g


def _params(*sem):
    return pltpu.CompilerParams(dimension_semantics=sem, vmem_limit_bytes=VMEM_LIMIT)


def _rope_tables(seq, dim):
    inv = 1.0 / (ROPE_THETA ** (jnp.arange(0, dim, 2, dtype=_F32) / dim))
    ang = jnp.arange(seq, dtype=_F32)[:, None] * inv[None, :]
    return jnp.cos(ang), jnp.sin(ang)


def _even_rope_tables(seq):
    cos, sin = _rope_tables(seq, HEAD_DIM)
    half = HEAD_DIM // 2
    lane = jnp.arange(LANES)
    first = (lane % HEAD_DIM) < half
    c = cos[:, lane % half]
    s = sin[:, lane % half]
    return c, jnp.where(first, -s, 0.0), jnp.where(first, 0.0, s)


def _mla_rope_tables(seq, scale):
    cos, sin = _rope_tables(seq, MLA_ROPE)
    half = MLA_ROPE // 2
    lane = jnp.arange(LANES)
    r = (lane - MLA_NOPE) % half
    in_rope = (lane >= MLA_NOPE) & (lane < MLA_NOPE + MLA_ROPE)
    first = (lane >= MLA_NOPE) & (lane < MLA_NOPE + half)
    second = (lane >= MLA_NOPE + half) & (lane < MLA_NOPE + MLA_ROPE)
    c, s = cos[:, r], sin[:, r]
    cq = jnp.where(lane < MLA_NOPE, 1.0, jnp.where(in_rope, c, 0.0)) * scale
    ck = jnp.where(in_rope, c, 0.0)
    sa = jnp.where(first, -s, 0.0)
    sb = jnp.where(second, s, 0.0)
    return cq, sa * scale, sb * scale, ck, sa, sb


def _rotate(z, c, sa, sb, half):
    return z * c + pltpu.roll(z, LANES - half, 1) * sa + pltpu.roll(z, half, 1) * sb


def _store_transposed(vt_ref, row0, z):
    vt_ref[row0:row0 + LANES, :] = z.T.astype(_BF16)


def _vt_spec(features, tm):
    per_slab = KV_TILE // tm
    return pl.BlockSpec((None, features, tm), lambda i: (i // per_slab, 0, i % per_slab))


def _even_proj_kernel(x_ref, g_ref, w_ref, c_ref, sa_ref, sb_ref, qk_ref, vt_ref):
    h = _rms(x_ref[...], g_ref[...]).astype(_BF16)
    c, sa, sb = c_ref[...], sa_ref[...], sb_ref[...]
    chunk = 512
    blocks = chunk // LANES
    qk_chunk = {0: 0, 1: 1, 3: 2, 4: 3}
    v_chunk = {2: 0, 5: 1}
    for ci in range(EVEN_IN // chunk):
        z = _dot(h, w_ref[:, ci * chunk:(ci + 1) * chunk])
        for b in range(blocks):
            zz = z[:, b * LANES:(b + 1) * LANES]
            if ci in v_chunk:
                _store_transposed(vt_ref, (v_chunk[ci] * blocks + b) * LANES, zz)
                continue
            zz = _rotate(zz, c, sa, sb, HEAD_DIM // 2)
            if ci in (0, 3):
                zz = zz * (HEAD_DIM ** -0.5 * LOG2E)
            col = (qk_chunk[ci] * blocks + b) * LANES
            qk_ref[:, col:col + LANES] = zz.astype(_BF16)


def _even_proj(x2, g, w, tabs, seq):
    n = x2.shape[0]
    tm = TOKEN_TILE
    spb = seq // tm
    tab_spec = pl.BlockSpec((tm, LANES), lambda i: (i % spb, 0))
    qk_w = 4 * 512
    v_w = 2 * 512
    return pl.pallas_call(
        _even_proj_kernel,
        grid=(n // tm,),
        in_specs=[
            pl.BlockSpec((tm, D_MODEL), lambda i: (i, 0)),
            pl.BlockSpec((1, D_MODEL), lambda i: (0, 0)),
            pl.BlockSpec((D_MODEL, EVEN_IN), lambda i: (0, 0)),
            tab_spec, tab_spec, tab_spec,
        ],
        out_specs=[pl.BlockSpec((tm, qk_w), lambda i: (i, 0)), _vt_spec(v_w, tm)],
        out_shape=[jax.ShapeDtypeStruct((n, qk_w), _BF16),
                   jax.ShapeDtypeStruct((n // KV_TILE, v_w, KV_TILE), _BF16)],
        compiler_params=_params("parallel"),
        name="even_proj",
    )(x2, g, w, *tabs)


class _AttnRefs:
    def __init__(self, s_ref, pm_ref, sd_ref, pmd_ref, m_ref, l_ref, acc_ref):
        self.main = [(s_ref, pm_ref, 0), (s_ref, pm_ref, 1)]
        self.diag = [(sd_ref, pmd_ref, 0), (sd_ref, pmd_ref, 1)]
        self.m, self.l, self.acc = m_ref, l_ref, acc_ref


def _attn_scratch(dv, tq):
    scores = pltpu.VMEM((2, 2, KV_TILE, tq), _F32)
    maxima = pltpu.VMEM((2, 2, 8, tq), _F32)
    return [scores, maxima, scores, maxima,
            pltpu.VMEM((2, 1, tq), _F32),
            pltpu.VMEM((2, 1, tq), _F32),
            pltpu.VMEM((2, dv, tq), _F32)]


def _kv_rows(c):
    return pl.ds(pl.multiple_of(c * KV_TILE, KV_TILE), KV_TILE)


def _sub_rows(c, r):
    return pl.ds(pl.multiple_of(c * KV_TILE + r * SUB_TILE, SUB_TILE), SUB_TILE)


def _fold8(x, op):
    return op(x.reshape(x.shape[0] // 8, 8, x.shape[1]), axis=0)


def _stage(refs, value, softmax_of=None, scores_of=None):
    chains = range(2)
    n_sub = KV_TILE // SUB_TILE
    n_in = n_out = 0
    if softmax_of is not None:
        (s_in, pm_in, b_in), c, n_in = softmax_of
        m_new, alpha, acc, lsum = [], [], [], [None, None]
        for t in chains:
            m_prev = refs.m[t]
            m_new.append(jnp.maximum(m_prev, jnp.max(pm_in[b_in, t], axis=0, keepdims=True)))
            alpha.append(jnp.exp2(m_prev - m_new[t]))
            acc.append(alpha[t] * refs.acc[t])
    if scores_of is not None:
        (s_out, pm_out, b_out), c_out, score, mask, n_out = scores_of
        pm = [None, None]
    for r in range(n_sub):
        rows = slice(r * SUB_TILE, (r + 1) * SUB_TILE)
        for t in chains:
            if r < n_out:
                s = score(t, c_out, r)
                if mask is not None:
                    s = mask(s, r)
                s_out[b_out, t, rows, :] = s
                part = _fold8(s, jnp.max)
                pm[t] = part if pm[t] is None else jnp.maximum(pm[t], part)
            if r < n_in:
                p = jnp.exp2(s_in[b_in, t, rows, :] - m_new[t])
                part = _fold8(p, jnp.sum)
                lsum[t] = part if lsum[t] is None else lsum[t] + part
                acc[t] = acc[t] + _dot(value(t, c, r), p.astype(_BF16))
    for t in chains:
        if scores_of is not None:
            pm_out[b_out, t] = pm[t]
        if softmax_of is not None:
            refs.m[t] = m_new[t]
            refs.l[t] = alpha[t] * refs.l[t] + jnp.sum(lsum[t], axis=0, keepdims=True)
            refs.acc[t] = acc[t]


def _sweep(refs, first_qi, make_score, value, finalize):
    tq = refs.m.shape[-1]
    n_sub = KV_TILE // SUB_TILE
    m0, m1 = refs.main
    delta = (lax.broadcasted_iota(jnp.int32, (SUB_TILE, tq), 0)
             - lax.broadcasted_iota(jnp.int32, (SUB_TILE, tq), 1))

    def plan(j):
        last = ((first_qi + j) * tq) // KV_TILE
        start = (j * tq) % KV_TILE
        n_diag = min(n_sub, (start + tq - 1) // SUB_TILE + 1)

        def causal(s, r):
            if (r + 1) * SUB_TILE - 1 <= start:
                return s
            return jnp.where(delta <= start - r * SUB_TILE, s, -jnp.inf)

        return last, n_diag, (refs.diag[j % 2], last, make_score(j), causal, n_diag)

    plans = [plan(j) for j in range(Q_GROUP)]
    _stage(refs, value, scores_of=plans[0][2])

    for j, (last, n_diag, (diag, _, score, _, _)) in enumerate(plans):
        refs.m[...] = jnp.full(refs.m.shape, -jnp.inf, _F32)
        refs.l[...] = jnp.zeros(refs.l.shape, _F32)
        refs.acc[...] = jnp.zeros(refs.acc.shape, _F32)
        odd = last % 2

        def below(buf, tile):
            return (buf, tile, score, None, n_sub)

        @pl.when(odd == 1)
        def _():
            _stage(refs, value, softmax_of=(diag, last, n_diag), scores_of=below(m1, 0))

        @pl.when(jnp.logical_and(odd == 0, last >= 2))
        def _():
            _stage(refs, value, softmax_of=(diag, last, n_diag), scores_of=below(m0, 0))
            _stage(refs, value, softmax_of=(m0, 0, n_sub), scores_of=below(m1, 1))

        base = 1 - odd

        def pair(k, carry):
            c = base + 2 * k
            _stage(refs, value, softmax_of=(m1, c, n_sub), scores_of=below(m0, c + 1))
            _stage(refs, value, softmax_of=(m0, c + 1, n_sub), scores_of=below(m1, c + 2))
            return carry

        lax.fori_loop(0, lax.shift_right_arithmetic(last - 1 - base, 1), pair, 0)

        nxt = plans[j + 1][2] if j + 1 < Q_GROUP else None

        @pl.when(last == 0)
        def _():
            _stage(refs, value, softmax_of=(diag, last, n_diag), scores_of=nxt)

        @pl.when(last > 0)
        def _():
            _stage(refs, value, softmax_of=(m1, last - 1, n_sub), scores_of=nxt)

        finalize(j)


def _transposed(q):
    return q.astype(_F32).T


def _moba_kernel(q_ref, k_ref, vt_ref, o_ref, km_ref, ka_ref, *scratch, nb, topk):
    step = pl.program_id(2)
    tq = Q_TILE
    half = HEAD_DIM // 2

    @pl.when(step == 0)
    def _():
        kf = k_ref[...].astype(_F32)
        means = jnp.sum(kf.reshape(nb, MOBA_BLOCK, LANES), axis=1) * (1.0 / MOBA_BLOCK)
        lane_b = lax.broadcasted_iota(jnp.int32, (nb, LANES), 1)
        km_ref[...] = jnp.zeros(km_ref.shape, _F32)
        km_ref[0:nb, :] = jnp.where(lane_b < HEAD_DIM, 0.0, means)
        km_ref[HEAD_DIM:HEAD_DIM + nb, :] = jnp.where(lane_b < HEAD_DIM, means, 0.0)

        lane = lax.broadcasted_iota(jnp.int32, (KV_TILE, LANES), 1)
        blk = lax.broadcasted_iota(jnp.int32, (KV_TILE, LANES), 0) // MOBA_BLOCK

        def build(c, carry):
            rows = _kv_rows(c)
            k = k_ref[rows, :].astype(_F32)
            b = blk + c * (KV_TILE // MOBA_BLOCK)
            ka_ref[0, rows, :] = jnp.where(lane < HEAD_DIM, k,
                                           jnp.where(lane == b + HEAD_DIM, 1.0, 0.0)).astype(_BF16)
            ka_ref[1, rows, :] = jnp.where(lane < HEAD_DIM,
                                           jnp.where(lane == b, 1.0, 0.0), k).astype(_BF16)
            return carry

        lax.fori_loop(0, k_ref.shape[0] // KV_TILE, build, 0)

    row = lax.broadcasted_iota(jnp.int32, (half, tq), 0).astype(_F32)
    z_half = jnp.zeros((half, tq), _F32)

    def make_score(j):
        qt = _transposed(q_ref[j * tq:(j + 1) * tq, :])
        gate = _dot(km_ref[...].astype(_BF16), qt.astype(_BF16))
        qpos = (step * Q_GROUP + j) * tq + lax.broadcasted_iota(jnp.int32, (half, tq), 1)
        own = (qpos // MOBA_BLOCK).astype(_F32)

        def block_bias(base):
            g = jnp.where(row < own, gate[base:base + half], -jnp.inf)
            bias = jnp.where(row == own, 0.0, MASK_BIAS)
            for _ in range(topk):
                mx = jnp.max(g, axis=0, keepdims=True)
                hit = (g == mx) & (mx > -jnp.inf)
                idx = jnp.min(jnp.where(hit, row, 4.0 * LANES), axis=0, keepdims=True)
                pick = row == idx
                bias = jnp.where(pick, 0.0, bias)
                g = jnp.where(pick, -jnp.inf, g)
            return bias

        qs = (jnp.concatenate([qt[:HEAD_DIM], block_bias(HEAD_DIM), z_half], axis=0).astype(_BF16),
              jnp.concatenate([block_bias(0), z_half, qt[HEAD_DIM:]], axis=0).astype(_BF16))
        return lambda t, c, r: _dot(ka_ref[t, _sub_rows(c, r), :], qs[t])

    def value(t, c, r):
        return vt_ref[c, t * HEAD_DIM:(t + 1) * HEAD_DIM, r * SUB_TILE:(r + 1) * SUB_TILE]

    refs = _AttnRefs(*scratch)

    def finalize(j):
        ot = jnp.concatenate([refs.acc[0] / refs.l[0], refs.acc[1] / refs.l[1]], axis=0)
        o_ref[j * tq:(j + 1) * tq, :] = ot.T.astype(o_ref.dtype)

    _sweep(refs, step * Q_GROUP, make_score, value, finalize)


def _moba(qk3, vt):
    bsz, seq, _ = qk3.shape
    tq = Q_TILE * Q_GROUP
    assert Q_TILE % MOBA_BLOCK == 0 and seq % KV_TILE == 0 and seq % tq == 0
    nb = seq // MOBA_BLOCK
    assert nb <= HEAD_DIM // 2
    topk = max(1, min(MOBA_TOPK, nb - 1))
    pairs = MOBA_HEADS // 2
    return pl.pallas_call(
        functools.partial(_moba_kernel, nb=nb, topk=topk),
        grid=(bsz, pairs, seq // tq),
        in_specs=[
            pl.BlockSpec((None, tq, LANES), lambda b, p, i: (b, i, p)),
            pl.BlockSpec((None, seq, LANES), lambda b, p, i: (b, 0, pairs + p)),
            pl.BlockSpec((seq // KV_TILE, LANES, KV_TILE), lambda b, p, i: (b, p, 0)),
        ],
        out_specs=pl.BlockSpec((None, tq, LANES), lambda b, p, i: (b, i, p)),
        out_shape=jax.ShapeDtypeStruct((bsz, seq, pairs * LANES), _BF16),
        scratch_shapes=[pltpu.VMEM((LANES, LANES), _F32), pltpu.VMEM((2, seq, LANES), _BF16)]
                       + _attn_scratch(HEAD_DIM, Q_TILE),
        compiler_params=_params("parallel", "parallel", "arbitrary"),
        name="moba_attn",
    )(qk3, qk3, vt)


def _diff_kernel(q_ref, k_ref, vt_ref, lam_ref, g_ref, o_ref, *scratch, lam_init):
    step = pl.program_id(2)
    tq = Q_TILE
    z_head = jnp.zeros((HEAD_DIM, tq), _F32)

    def make_score(j):
        qt = _transposed(q_ref[j * tq:(j + 1) * tq, :])
        qd = (jnp.concatenate([qt[:HEAD_DIM], z_head], axis=0).astype(_BF16),
              jnp.concatenate([z_head, qt[HEAD_DIM:]], axis=0).astype(_BF16))
        return lambda t, c, r: _dot(k_ref[_sub_rows(c, r), :], qd[t])

    def value(t, c, r):
        return vt_ref[c, :, r * SUB_TILE:(r + 1) * SUB_TILE]

    lv = lam_ref[...]
    lam = (jnp.exp(jnp.sum(lv[0:1] * lv[1:2], axis=1, keepdims=True))
           - jnp.exp(jnp.sum(lv[2:3] * lv[3:4], axis=1, keepdims=True)) + lam_init)
    refs = _AttnRefs(*scratch)

    def finalize(j):
        ot = refs.acc[0] / refs.l[0] - lam * (refs.acc[1] / refs.l[1])
        o_ref[j * tq:(j + 1) * tq, :] = (_rms(ot.T, g_ref[...]) * (1.0 - lam_init)).astype(o_ref.dtype)

    _sweep(refs, step * Q_GROUP, make_score, value, finalize)


def _diff(qk3, vt, lam_rows, subln, lam_init):
    bsz, seq, _ = qk3.shape
    tq = Q_TILE * Q_GROUP
    base = 2 * MOBA_HEADS * HEAD_DIM // LANES
    vbase = MOBA_HEADS * HEAD_DIM // LANES
    return pl.pallas_call(
        functools.partial(_diff_kernel, lam_init=lam_init),
        grid=(bsz, DIFF_HEADS, seq // tq),
        in_specs=[
            pl.BlockSpec((None, tq, LANES), lambda b, h, i: (b, i, base + h)),
            pl.BlockSpec((None, seq, LANES), lambda b, h, i: (b, 0, base + DIFF_HEADS + h)),
            pl.BlockSpec((seq // KV_TILE, LANES, KV_TILE), lambda b, h, i: (b, vbase + h, 0)),
            pl.BlockSpec((4, HEAD_DIM), lambda b, h, i: (0, 0)),
            pl.BlockSpec((1, LANES), lambda b, h, i: (0, 0)),
        ],
        out_specs=pl.BlockSpec((None, tq, LANES), lambda b, h, i: (b, i, h)),
        out_shape=jax.ShapeDtypeStruct((bsz, seq, DIFF_HEADS * LANES), _BF16),
        scratch_shapes=_attn_scratch(LANES, Q_TILE),
        compiler_params=_params("parallel", "parallel", "arbitrary"),
        name="diff_attn",
    )(qk3, qk3, vt, lam_rows, subln)


def _mla_proj_kernel(dest_ref, next_ref, x_ref, ys_ref,
                     g_ref, w1_ref, qg_ref, kg_ref, wq_ref, wk_ref, wv_ref,
                     cq_ref, saq_ref, sbq_ref, ck_ref, sak_ref, sbk_ref,
                     x_out, q_out, k_out, vt_out, buf_ref, sems):
    x = x_ref[...] + _gather_rows(dest_ref, next_ref, ys_ref, buf_ref, sems)
    x_out[...] = x
    h = _rms(x, g_ref[...]).astype(_BF16)
    z = _dot(h, w1_ref[...])
    cq = _rms(z[:, :MLA_Q_RANK], qg_ref[...]).astype(_BF16)
    ckv = _rms(z[:, MLA_Q_RANK:MLA_Q_RANK + MLA_KV_RANK], kg_ref[...]).astype(_BF16)
    kr = z[:, MLA_Q_RANK + MLA_KV_RANK:]
    half = MLA_ROPE // 2
    kr = _rotate(kr, ck_ref[...], sak_ref[...], sbk_ref[...], half)
    qf = _dot(cq, wq_ref[...])
    kf = _dot(ckv, wk_ref[...])
    cqt, saq, sbq = cq_ref[...], saq_ref[...], sbq_ref[...]
    for hd in range(MLA_HEADS):
        sl = slice(hd * LANES, (hd + 1) * LANES)
        q_out[:, sl] = _rotate(qf[:, sl], cqt, saq, sbq, half).astype(_BF16)
        k_out[:, sl] = (kf[:, sl] + kr).astype(_BF16)
    vf = _dot(ckv, wv_ref[...])
    for b in range(MLA_HEADS * MLA_V // LANES):
        _store_transposed(vt_out, b * LANES, vf[:, b * LANES:(b + 1) * LANES])


def _mla_proj(dest, x2, ys, g, w1, qg, kg, wq, wk, wv, tabs, seq):
    n = x2.shape[0]
    tm = TOKEN_TILE
    spb = seq // tm
    hw = MLA_HEADS * LANES
    vw = MLA_HEADS * MLA_V
    tab_spec = pl.BlockSpec((tm, LANES), lambda i: (i % spb, 0))

    def full(a):
        return pl.BlockSpec(a.shape, lambda i: (0,) * a.ndim)

    return pl.pallas_call(
        _mla_proj_kernel,
        grid=(n // tm,),
        in_specs=[_index_spec(tm), _index_spec(tm, n // tm - 1),
                  pl.BlockSpec((tm, D_MODEL), lambda i: (i, 0)),
                  pl.BlockSpec(memory_space=pl.ANY),
                  full(g), full(w1), full(qg), full(kg), full(wq), full(wk), full(wv)]
                 + [tab_spec] * 6,
        out_specs=[pl.BlockSpec((tm, D_MODEL), lambda i: (i, 0)),
                   pl.BlockSpec((tm, hw), lambda i: (i, 0)),
                   pl.BlockSpec((tm, hw), lambda i: (i, 0)),
                   _vt_spec(vw, tm)],
        out_shape=[jax.ShapeDtypeStruct((n, D_MODEL), _F32),
                   jax.ShapeDtypeStruct((n, hw), _BF16),
                   jax.ShapeDtypeStruct((n, hw), _BF16),
                   jax.ShapeDtypeStruct((n // KV_TILE, vw, KV_TILE), _BF16)],
        scratch_shapes=_gather_scratch(tm),
        compiler_params=_params("arbitrary"),
        name="mla_proj",
    )(dest, dest, x2, ys, g, w1, qg, kg, wq, wk, wv, *tabs)


def _mla_kernel(q_ref, k_ref, vt_ref, o_ref, *scratch):
    step = pl.program_id(2)
    tq = Q_TILE

    def make_score(j):
        q = q_ref[j * tq:(j + 1) * tq, :]
        qd = tuple(_transposed(q[:, t * LANES:(t + 1) * LANES]).astype(_BF16) for t in range(2))
        return lambda t, c, r: _dot(k_ref[_sub_rows(c, r), t * LANES:(t + 1) * LANES], qd[t])

    def value(t, c, r):
        return vt_ref[c, t * MLA_V:(t + 1) * MLA_V, r * SUB_TILE:(r + 1) * SUB_TILE]

    refs = _AttnRefs(*scratch)

    def finalize(j):
        ot = jnp.concatenate([refs.acc[0] / refs.l[0], refs.acc[1] / refs.l[1]], axis=0)
        o_ref[j * tq:(j + 1) * tq, :] = ot.T.astype(o_ref.dtype)

    _sweep(refs, step * Q_GROUP, make_score, value, finalize)


def _mla(q3, k3, vt):
    bsz, seq, _ = q3.shape
    tq = Q_TILE * Q_GROUP
    pairs = MLA_HEADS // 2
    return pl.pallas_call(
        _mla_kernel,
        grid=(bsz, pairs, seq // tq),
        in_specs=[
            pl.BlockSpec((None, tq, 2 * LANES), lambda b, p, i: (b, i, p)),
            pl.BlockSpec((None, seq, 2 * LANES), lambda b, p, i: (b, 0, p)),
            pl.BlockSpec((seq // KV_TILE, LANES, KV_TILE), lambda b, p, i: (b, p, 0)),
        ],
        out_specs=pl.BlockSpec((None, tq, LANES), lambda b, p, i: (b, i, p)),
        out_shape=jax.ShapeDtypeStruct((bsz, seq, pairs * LANES), _BF16),
        scratch_shapes=_attn_scratch(MLA_V, Q_TILE),
        compiler_params=_params("parallel", "parallel", "arbitrary"),
        name="mla_attn",
    )(q3, k3, vt)


def _route(logits):
    lane = lax.broadcasted_iota(jnp.int32, logits.shape, 1).astype(_F32)
    big = 4.0 * LANES
    is_group = (lane >= N_EXPERTS) & (lane < N_EXPERTS + N_GROUPS)
    gl = jnp.where(is_group, logits, -jnp.inf)
    gmax = jnp.max(gl, axis=1, keepdims=True)
    grp = jnp.min(jnp.where(gl == gmax, lane, big), axis=1, keepdims=True) - N_EXPERTS
    p_group = 1.0 / jnp.sum(jnp.exp(gl - gmax), axis=1, keepdims=True)
    in_group = (lane >= grp * EXPERTS_PER_GROUP) & (lane < (grp + 1) * EXPERTS_PER_GROUP)
    el = jnp.where(in_group, logits, -jnp.inf)
    e1 = jnp.max(el, axis=1, keepdims=True)
    i1 = jnp.min(jnp.where(el == e1, lane, big), axis=1, keepdims=True)
    el2 = jnp.where(lane == i1, -jnp.inf, el)
    e2 = jnp.max(el2, axis=1, keepdims=True)
    i2 = jnp.min(jnp.where(el2 == e2, lane, big), axis=1, keepdims=True)
    r = jnp.exp(e2 - e1)
    w1 = p_group / (1.0 + r)
    w2 = w1 * r
    first = i1 < i2
    a = jnp.minimum(i1, i2) - grp * EXPERTS_PER_GROUP
    b = jnp.maximum(i1, i2) - grp * EXPERTS_PER_GROUP
    cls = grp * PAIRS_PER_GROUP + a * (2 * EXPERTS_PER_GROUP - 1 - a) * 0.5 + (b - a - 1.0)
    return jnp.where(first, w1, w2), jnp.where(first, w2, w1), cls


_HIGH_HALF = -65536


def _pack_pairs(x):
    bits = lax.bitcast_convert_type(x.astype(_BF16).astype(_F32), jnp.int32)
    w = x.shape[1] // 2
    return lax.shift_right_logical(bits[:, :w], 16) | (bits[:, w:] & _HIGH_HALF)


def _unpack_pairs(words):
    lo = lax.bitcast_convert_type(lax.shift_left(words, 16), _F32)
    hi = lax.bitcast_convert_type(words & _HIGH_HALF, _F32)
    return jnp.concatenate([lo, hi], axis=1).astype(_BF16)


def _lane_dense(col, eye):
    t = col.shape[0]
    return jnp.sum(jnp.where(eye, col, 0.0).reshape(t // LANES, LANES, LANES), axis=1)


def _out_proj_kernel(*refs, n_mix):
    x_ref = refs[0]
    mix = refs[1:1 + 2 * n_mix]
    g_ref, wr_ref, br_ref = refs[1 + 2 * n_mix:4 + 2 * n_mix]
    x_out, row_out, cls_out, rank_out, count_out, count_ref = refs[4 + 2 * n_mix:]

    @pl.when(pl.program_id(0) == 0)
    def _():
        count_ref[...] = jnp.zeros(count_ref.shape, _F32)

    x = x_ref[...]
    for i in range(n_mix):
        x = x + _dot(mix[2 * i][...], mix[2 * i + 1][...])
    x_out[...] = x
    t = _rms(x, g_ref[...])
    w_lo, w_hi, cls = _route(_dot(t.astype(_BF16), wr_ref[...]) + br_ref[...])

    tm = x.shape[0]
    lane = lax.broadcasted_iota(jnp.int32, (tm, LANES), 1)
    gate_blk = jnp.where(lane == 0, w_lo, 0.0) + jnp.where(lane == 1, w_hi, 0.0)
    row_out[:, :D_MODEL // 2] = _pack_pairs(t)
    row_out[:, D_MODEL // 2:] = lax.bitcast_convert_type(gate_blk, jnp.int32)

    onehot = jnp.where(lane.astype(_F32) == cls, 1.0, 0.0)
    earlier = (lax.broadcasted_iota(jnp.int32, (tm, tm), 1)
               < lax.broadcasted_iota(jnp.int32, (tm, tm), 0))
    before = _dot(jnp.where(earlier, 1.0, 0.0).astype(_BF16), onehot.astype(_BF16))
    rank = jnp.sum(onehot * (before + count_ref[...]), axis=1, keepdims=True)
    count_ref[...] += jnp.sum(onehot, axis=0, keepdims=True)
    count_out[...] = count_ref[...].astype(jnp.int32)
    eye = lax.broadcasted_iota(jnp.int32, (tm, LANES), 0) % LANES == lane
    cls_out[...] = _lane_dense(cls, eye).astype(jnp.int32)
    rank_out[...] = _lane_dense(rank, eye).astype(jnp.int32)


def _out_proj(x2, mixes, g, wr, br):
    n = x2.shape[0]
    tm = TOKEN_TILE
    in_specs = [pl.BlockSpec((tm, D_MODEL), lambda i: (i, 0))]
    args = [x2]
    for o, w in mixes:
        in_specs.append(pl.BlockSpec((tm, o.shape[1]), lambda i: (i, 0)))
        in_specs.append(pl.BlockSpec(w.shape, lambda i: (0, 0)))
        args += [o, w]
    for a in (g, wr, br):
        in_specs.append(pl.BlockSpec(a.shape, lambda i: (0, 0)))
        args.append(a)
    idx_spec = pl.BlockSpec((None, tm // LANES, LANES), lambda i: (i, 0, 0))
    idx_shape = jax.ShapeDtypeStruct((n // tm, tm // LANES, LANES), jnp.int32)
    return pl.pallas_call(
        functools.partial(_out_proj_kernel, n_mix=len(mixes)),
        grid=(n // tm,),
        in_specs=in_specs,
        out_specs=[pl.BlockSpec((tm, D_MODEL), lambda i: (i, 0)),
                   pl.BlockSpec((tm, ROW_W), lambda i: (i, 0)),
                   idx_spec, idx_spec,
                   pl.BlockSpec((1, LANES), lambda i: (0, 0))],
        out_shape=[jax.ShapeDtypeStruct((n, D_MODEL), _F32),
                   jax.ShapeDtypeStruct((n, ROW_W), jnp.int32),
                   idx_shape, idx_shape,
                   jax.ShapeDtypeStruct((1, LANES), jnp.int32)],
        scratch_shapes=[pltpu.VMEM((1, LANES), _F32)],
        compiler_params=_params("arbitrary"),
        name="out_proj_route",
    )(*args)


def _class_experts():
    pairs = [(a, b) for a in range(EXPERTS_PER_GROUP) for b in range(a + 1, EXPERTS_PER_GROUP)]
    lo = [g * EXPERTS_PER_GROUP + a for g in range(N_GROUPS) for a, _ in pairs]
    hi = [g * EXPERTS_PER_GROUP + b for g in range(N_GROUPS) for _, b in pairs]
    return jnp.asarray(lo, jnp.int32), jnp.asarray(hi, jnp.int32)


def _tile_plan(counts, max_tiles):
    tiles = (counts + EXPERT_TILE - 1) // EXPERT_TILE
    ends = jnp.cumsum(tiles)
    offsets = (ends - tiles) * EXPERT_TILE
    tile = jnp.arange(max_tiles, dtype=jnp.int32)
    tile_cls = jnp.sum((ends[None, :] <= tile[:, None]).astype(jnp.int32), axis=1)
    tile_cls = jnp.minimum(tile_cls, N_CLASSES - 1)
    lo, hi = _class_experts()
    return offsets.astype(jnp.int32), lo[tile_cls], hi[tile_cls], ends[-1:].astype(jnp.int32)


def _sorted_rows_kernel(cls_ref, rank_ref, off_ref, dest_ref):
    table = jnp.broadcast_to(off_ref[...], cls_ref.shape)
    dest_ref[...] = jnp.take_along_axis(table, cls_ref[...], axis=1) + rank_ref[...]


def _sorted_rows(cls, rank, offsets):
    spec = pl.BlockSpec((None,) + cls.shape[1:], lambda i: (i, 0, 0))
    return pl.pallas_call(
        _sorted_rows_kernel,
        grid=(cls.shape[0],),
        in_specs=[spec, spec, pl.BlockSpec((1, LANES), lambda i: (0, 0))],
        out_specs=spec,
        out_shape=jax.ShapeDtypeStruct(cls.shape, jnp.int32),
        compiler_params=_params("parallel"),
        name="moe_sorted_rows",
    )(cls, rank, offsets[None, :])


def _row_copies(dest_ref, copy, method):
    tiles_per_row = LANES // SUBLANES
    for a in range(dest_ref.shape[0]):
        def body(g, carry, a=a):
            for j in range(SUBLANES):
                d = dest_ref[a, g * SUBLANES + j]
                getattr(copy(a * tiles_per_row + g, j, d), method)()
            return carry

        lax.fori_loop(0, tiles_per_row, body, 0)


def _dispatch_kernel(dest_ref, row_ref, xs_in_ref, xs_ref, sem):
    del xs_in_ref

    def copy(tile, j, d):
        return pltpu.make_async_copy(row_ref.at[tile, pl.ds(j, 1), :],
                                     xs_ref.at[pl.ds(d, 1), :], sem)

    _row_copies(dest_ref, copy, "start")
    _row_copies(dest_ref, copy, "wait")


def _index_spec(tm, last_step=None):
    if last_step is None:
        index = lambda i: (i, 0, 0)
    else:
        index = lambda i: (jnp.minimum(i + 1, last_step), 0, 0)
    return pl.BlockSpec((None, tm // LANES, LANES), index, memory_space=pltpu.SMEM)


def _dispatch(dest, rows, max_tiles):
    n = rows.shape[0]
    tm = TOKEN_TILE
    xs0 = jnp.zeros((max_tiles * EXPERT_TILE, ROW_W), rows.dtype)
    return pl.pallas_call(
        _dispatch_kernel,
        grid=(n // tm,),
        in_specs=[_index_spec(tm),
                  pl.BlockSpec((tm // SUBLANES, SUBLANES, ROW_W), lambda i: (i, 0, 0)),
                  pl.BlockSpec(memory_space=pl.ANY)],
        out_specs=pl.BlockSpec(memory_space=pl.ANY),
        out_shape=jax.ShapeDtypeStruct(xs0.shape, xs0.dtype),
        scratch_shapes=[pltpu.SemaphoreType.DMA(())],
        input_output_aliases={2: 0},
        compiler_params=_params("arbitrary"),
        name="moe_dispatch",
    )(dest, rows.reshape(n // SUBLANES, SUBLANES, ROW_W), xs0)


def _expert_kernel(lo_ref, hi_ref, used_ref, xs_ref, wg0, wu0, wd0, wg1, wu1, wd1, ys_ref):
    del lo_ref, hi_ref
    i = pl.program_id(0)

    @pl.when(i < used_ref[0])
    def _():
        xs = xs_ref[...]
        x = _unpack_pairs(xs[:, :D_MODEL // 2])
        gates = lax.bitcast_convert_type(xs[:, D_MODEL // 2:], _F32)
        ups = [(_dot(x, wg[...]), _dot(x, wu[...])) for wg, wu in ((wg0, wu0), (wg1, wu1))]
        acts = [(hg * (1.0 / (1.0 + jnp.exp(-hg))) * hu * gates[:, k:k + 1]).astype(_BF16)
                for k, (hg, hu) in enumerate(ups)]
        ys_ref[...] = _dot(acts[0], wd0[...]) + _dot(acts[1], wd1[...])

    @pl.when(i >= used_ref[0])
    def _():
        ys_ref[...] = jnp.zeros(ys_ref.shape, _F32)


def _experts(xs, tile_lo, tile_hi, used, wg, wu, wd):
    max_tiles = xs.shape[0] // EXPERT_TILE
    up = (None, D_MODEL, EXPERT_FF)
    down = (None, EXPERT_FF, D_MODEL)
    by_lo = lambda i, lo, hi, used: (lo[i], 0, 0)
    by_hi = lambda i, lo, hi, used: (hi[i], 0, 0)
    grid_spec = pltpu.PrefetchScalarGridSpec(
        num_scalar_prefetch=3,
        grid=(max_tiles,),
        in_specs=[pl.BlockSpec((EXPERT_TILE, ROW_W),
                               lambda i, lo, hi, used: (jnp.minimum(i, used[0] - 1), 0)),
                  pl.BlockSpec(up, by_lo), pl.BlockSpec(up, by_lo), pl.BlockSpec(down, by_lo),
                  pl.BlockSpec(up, by_hi), pl.BlockSpec(up, by_hi), pl.BlockSpec(down, by_hi)],
        out_specs=pl.BlockSpec((EXPERT_TILE, D_MODEL), lambda i, lo, hi, used: (i, 0)),
    )
    return pl.pallas_call(
        _expert_kernel,
        grid_spec=grid_spec,
        out_shape=jax.ShapeDtypeStruct((xs.shape[0], D_MODEL), _F32),
        compiler_params=_params("arbitrary"),
        name="moe_experts",
    )(tile_lo, tile_hi, used, xs, wg, wu, wd, wg, wu, wd)


def _gather_rows(dest_ref, next_ref, ys_ref, buf_ref, sems):
    step = pl.program_id(0)
    slot = step % 2

    def copies(half):
        def copy(tile, j, d):
            return pltpu.make_async_copy(ys_ref.at[pl.ds(d, 1), :],
                                         buf_ref.at[half, tile, pl.ds(j, 1), :], sems.at[half])
        return copy

    @pl.when(step == 0)
    def _():
        _row_copies(dest_ref, copies(0), "start")

    @pl.when(step + 1 < pl.num_programs(0))
    def _():
        _row_copies(next_ref, copies(1 - slot), "start")

    _row_copies(dest_ref, copies(slot), "wait")
    return buf_ref[slot].reshape(buf_ref.shape[1] * SUBLANES, buf_ref.shape[3])


def _gather_scratch(tm):
    return [pltpu.VMEM((2, tm // SUBLANES, SUBLANES, D_MODEL), _F32),
            pltpu.SemaphoreType.DMA((2,))]


def _final_kernel(dest_ref, next_ref, x_ref, g_ref, ys_ref, o_ref, buf_ref, sems):
    rows = _gather_rows(dest_ref, next_ref, ys_ref, buf_ref, sems)
    o_ref[...] = _rms(x_ref[...] + rows, g_ref[...])


def _final(dest, x2, g, ys):
    n = x2.shape[0]
    tm = TOKEN_TILE
    return pl.pallas_call(
        _final_kernel,
        grid=(n // tm,),
        in_specs=[_index_spec(tm), _index_spec(tm, n // tm - 1),
                  pl.BlockSpec((tm, D_MODEL), lambda i: (i, 0)),
                  pl.BlockSpec((1, D_MODEL), lambda i: (0, 0)),
                  pl.BlockSpec(memory_space=pl.ANY)],
        out_specs=pl.BlockSpec((tm, D_MODEL), lambda i: (i, 0)),
        out_shape=jax.ShapeDtypeStruct((n, D_MODEL), _F32),
        scratch_shapes=_gather_scratch(tm),
        compiler_params=_params("arbitrary"),
        name="moe_combine_norm",
    )(dest, dest, x2, g, ys)


def _moe(rows, cls, rank, counts, wg, wu, wd):
    n = rows.shape[0]
    max_tiles = n // EXPERT_TILE + N_CLASSES
    offsets, tile_lo, tile_hi, used = _tile_plan(counts[0], max_tiles)
    dest = _sorted_rows(cls, rank, offsets)
    xs = _dispatch(dest, rows, max_tiles)
    return _experts(xs, tile_lo, tile_hi, used, wg, wu, wd), dest


def _router_weights(w_group, b_group, w_expert, b_expert):
    pad = LANES - N_EXPERTS - N_GROUPS
    wr = jnp.concatenate([w_expert, w_group, jnp.zeros((D_MODEL, pad), _F32)], axis=1)
    br = jnp.concatenate([b_expert, b_group, jnp.zeros((pad,), _F32)])[None, :]
    return wr.astype(_BF16), br


def _mla_weights(w_in, w_q_up, w_kv_up):
    lat = MLA_Q_RANK + MLA_KV_RANK
    w1 = jnp.concatenate([w_in[:, :lat], jnp.zeros((D_MODEL, MLA_NOPE), _F32), w_in[:, lat:],
                          jnp.zeros((D_MODEL, LANES - MLA_NOPE - MLA_ROPE), _F32)], axis=1)
    qd = MLA_NOPE + MLA_ROPE
    wq = jnp.pad(w_q_up.reshape(MLA_Q_RANK, MLA_HEADS, qd), ((0, 0), (0, 0), (0, LANES - qd)))
    kv = w_kv_up.reshape(MLA_KV_RANK, MLA_HEADS, MLA_NOPE + MLA_V)
    wk = jnp.pad(kv[:, :, :MLA_NOPE], ((0, 0), (0, 0), (0, LANES - MLA_NOPE)))
    wv = kv[:, :, MLA_NOPE:]
    return (w1.astype(_BF16), wq.reshape(MLA_Q_RANK, -1).astype(_BF16),
            wk.reshape(MLA_KV_RANK, -1).astype(_BF16), wv.reshape(MLA_KV_RANK, -1).astype(_BF16))


def kernel(x, attn_norm, ev_w_in, ev_lambda_q1, ev_lambda_k1, ev_lambda_q2, ev_lambda_k2, ev_subln, ev_w_out, od_w_in, od_q_norm, od_kv_norm, od_w_q_up, od_w_kv_up, od_w_out, ffn_norm, moe_w_group, moe_b_group, moe_w_expert, moe_b_expert, moe_w_gate, moe_w_up, moe_w_down, final_norm):
    bsz, seq, d = x.shape
    n = bsz * seq
    x2 = x.reshape(n, d)

    lam_init = 0.8 - 0.6 * math.exp(-0.3 * 0)
    qk, vt = _even_proj(x2, attn_norm[0][None, :], ev_w_in[0].astype(_BF16),
                        _even_rope_tables(seq), seq)
    qk3 = qk.reshape(bsz, seq, -1)
    o_a = _moba(qk3, vt).reshape(n, -1)
    lam_rows = jnp.stack([ev_lambda_q1[0], ev_lambda_k1[0], ev_lambda_q2[0], ev_lambda_k2[0]])
    o_b = _diff(qk3, vt, lam_rows, ev_subln[0][None, :], lam_init).reshape(n, -1)
    w_out = ev_w_out[0].astype(_BF16)
    wa, wb = w_out[:o_a.shape[1]], w_out[o_a.shape[1]:]
    wr, br = _router_weights(moe_w_group[0], moe_b_group[0], moe_w_expert[0], moe_b_expert[0])
    x2, rows, cls, rank, counts = _out_proj(x2, [(o_a, wa), (o_b, wb)], ffn_norm[0][None, :],
                                            wr, br)
    ys, dest = _moe(rows, cls, rank, counts, moe_w_gate[0].astype(_BF16),
                    moe_w_up[0].astype(_BF16), moe_w_down[0].astype(_BF16))

    w1, wq, wk, wv = _mla_weights(od_w_in[0], od_w_q_up[0], od_w_kv_up[0])
    tabs = _mla_rope_tables(seq, (MLA_NOPE + MLA_ROPE) ** -0.5 * LOG2E)
    x2, q, k, vt = _mla_proj(dest, x2, ys, attn_norm[1][None, :], w1,
                             od_q_norm[0][None, :], od_kv_norm[0][None, :], wq, wk, wv, tabs, seq)
    o_c = _mla(q.reshape(bsz, seq, -1), k.reshape(bsz, seq, -1), vt)
    wr, br = _router_weights(moe_w_group[1], moe_b_group[1], moe_w_expert[1], moe_b_expert[1])
    x2, rows, cls, rank, counts = _out_proj(x2, [(o_c.reshape(n, -1), od_w_out[0].astype(_BF16))],
                                            ffn_norm[1][None, :], wr, br)
    ys, dest = _moe(rows, cls, rank, counts, moe_w_gate[1].astype(_BF16),
                    moe_w_up[1].astype(_BF16), moe_w_down[1].astype(_BF16))
    out = _final(dest, x2, final_norm[None, :], ys)
    return out.reshape(bsz, seq, d)
```
